```python
import jax, jax.numpy as jnp
from jax import lax
import numpy as np

D_MODEL = 1024
BATCH = 16
SEQ = 2048
DEPTH = 1

CTX_LEN = 256
GRID_W = 64
N_HEADS = 8
HEAD_DIM = 64
ATT_W = N_HEADS * HEAD_DIM
CONV_W = 512
WIN_H_MAX = 8
WIN_W = 16
Q_COLS = 16
K_COLS = 32
D_FF = 2816
CONV_K = 3
EPS = 1e-6
NEG_INF = -1e30

Q_OFF = 0
K_OFF = ATT_W
V_OFF = 2 * ATT_W
BG_OFF = 3 * ATT_W
CG_OFF = BG_OFF + CONV_W
HV_OFF = CG_OFF + CONV_W
GA_OFF = HV_OFF + CONV_W
GB_OFF = GA_OFF + D_MODEL
IN_W = GB_OFF + D_MODEL

kernel_name = "hybrid_natten_shortconv_convffn_dit"


def rmsnorm(x, g):
    x32 = x.astype(jnp.float32)
    y = x32 * lax.rsqrt(jnp.mean(x32 * x32, axis=-1, keepdims=True) + EPS)
    return (y * g.astype(jnp.float32)).astype(x.dtype)


def modulate(h, shift, scale):
    return h * (1 + scale) + shift


def heads(t):
    return t.reshape(*t.shape[:-1], N_HEADS, HEAD_DIM)


def dwconv3(x, w):
    ch = x.shape[-1]
    return lax.conv_general_dilated(
        x, w[:, None, :].astype(x.dtype), window_strides=(1,), padding=((1, 1),),
        dimension_numbers=("NWC", "WIO", "NWC"), feature_group_count=ch)


def _col_blocks():
    n_cb = GRID_W // Q_COLS
    qcol = np.arange(GRID_W).reshape(n_cb, Q_COLS)
    cs = np.clip(qcol - WIN_W // 2, 0, GRID_W - WIN_W)
    kstart = np.clip(cs[:, 0], 0, GRID_W - K_COLS)
    kcol = kstart[:, None] + np.arange(K_COLS)
    valid = (kcol[:, None, :] >= cs[:, :, None]) & (kcol[:, None, :] < cs[:, :, None] + WIN_W)
    dc = np.clip(kcol[:, None, :] - qcol[:, :, None] + WIN_W - 1, 0, 2 * WIN_W - 2)
    return kcol, valid, dc


def neighbourhood_attention(q, k, v, kc, vc, rpb):
    b, s = q.shape[:2]
    rows = s // GRID_W
    kh = min(WIN_H_MAX, rows)
    n_cb = GRID_W // Q_COLS
    kcol, valid, dc = _col_blocks()
    mask = jnp.asarray(valid)[None, :, :, None, :]
    grid = lambda t: t.reshape(b, rows, GRID_W, N_HEADS, HEAD_DIM)
    qg = jnp.moveaxis(grid(q * (HEAD_DIM ** -0.5)), 1, 0)
    kg, vg = grid(k), grid(v)

    def one_row(args):
        r, q_row = args
        rs = jnp.clip(r - kh // 2, 0, rows - kh)
        k_rows = lax.dynamic_slice_in_dim(kg, rs, kh, axis=1)
        v_rows = lax.dynamic_slice_in_dim(vg, rs, kh, axis=1)
        kb = k_rows[:, :, kcol]
        vb = v_rows[:, :, kcol]
        qb = q_row.reshape(b, n_cb, Q_COLS, N_HEADS, HEAD_DIM)
        dr = rs + jnp.arange(kh) - r + WIN_H_MAX - 1
        bias = rpb[:, dr][:, :, dc]
        bias = jnp.where(mask, jnp.transpose(bias, (0, 2, 3, 1, 4)).astype(jnp.float32), NEG_INF)
        s_win = jnp.einsum("bnqhd,brnkhd->bhnqrk", qb, kb).astype(jnp.float32) + bias
        s_win = s_win.reshape(b, N_HEADS, n_cb, Q_COLS, kh * K_COLS)
        s_ctx = jnp.einsum("bnqhd,bjhd->bhnqj", qb, kc).astype(jnp.float32)
        p = jax.nn.softmax(jnp.concatenate([s_win, s_ctx], axis=-1), axis=-1)
        p_win = p[..., :kh * K_COLS].reshape(b, N_HEADS, n_cb, Q_COLS, kh, K_COLS).astype(v.dtype)
        p_ctx = p[..., kh * K_COLS:].astype(v.dtype)
        o = (jnp.einsum("bhnqrk,brnkhd->bnqhd", p_win, vb)
             + jnp.einsum("bhnqj,bjhd->bnqhd", p_ctx, vc))
        return o.reshape(b, GRID_W, N_HEADS, HEAD_DIM)

    out = lax.map(one_row, (jnp.arange(rows), qg))
    return jnp.moveaxis(out, 0, 1).reshape(b, s, ATT_W)


def context_attention(qc, kc, vc):
    b, n = qc.shape[:2]
    s = jnp.einsum("bihd,bjhd->bhij", qc * (HEAD_DIM ** -0.5), kc).astype(jnp.float32)
    p = jax.nn.softmax(s, axis=-1).astype(vc.dtype)
    return jnp.einsum("bhij,bjhd->bihd", p, vc).reshape(b, n, ATT_W)


def merge_branches(p, attn, w_conv, w_ba, w_bb, w_o):
    bg, cg, hv = p[..., BG_OFF:CG_OFF], p[..., CG_OFF:HV_OFF], p[..., HV_OFF:GA_OFF]
    ga, gb = p[..., GA_OFF:GB_OFF], p[..., GB_OFF:IN_W]
    y_attn = attn @ w_ba
    y_conv = (bg * dwconv3(cg * hv, w_conv)) @ w_bb
    return (jax.nn.sigmoid(ga) * y_attn + jax.nn.sigmoid(gb) * y_conv) @ w_o


def conv_ffn(h, w_up, w_conv, b_conv, w_down):
    u = h @ w_up
    a, g = u[..., :D_FF], u[..., D_FF:]
    return (jax.nn.gelu(dwconv3(g, w_conv) + b_conv, approximate=True) * a) @ w_down


def setup_inputs(seed: int = 0) -> dict:
    key = jax.random.key(seed)
    ks = jax.random.split(key, 20)
    n = jax.random.normal
    f32 = jnp.float32
    return {
        "x": n(ks[0], (BATCH, SEQ, D_MODEL), f32),
        "c": n(ks[1], (BATCH, D_MODEL), f32),
        "ctx": n(ks[2], (BATCH, CTX_LEN, D_MODEL), f32),
        "c_ctx": n(ks[3], (D_MODEL,), f32),
        "w_ada": n(ks[4], (DEPTH, D_MODEL, 6 * D_MODEL), f32) * (0.5 * D_MODEL ** -0.5),
        "b_ada": n(ks[5], (DEPTH, 6 * D_MODEL), f32) * 0.01,
        "g_pre_mix": 1.0 + 0.02 * n(ks[6], (DEPTH, D_MODEL), f32),
        "g_post_mix": 1.0 + 0.02 * n(ks[7], (DEPTH, D_MODEL), f32),
        "g_pre_ffn": 1.0 + 0.02 * n(ks[8], (DEPTH, D_MODEL), f32),
        "g_post_ffn": 1.0 + 0.02 * n(ks[9], (DEPTH, D_MODEL), f32),
        "w_in": n(ks[10], (DEPTH, D_MODEL, IN_W), f32) * D_MODEL ** -0.5,
        "rpb": n(ks[11], (DEPTH, N_HEADS, 2 * WIN_H_MAX - 1, 2 * WIN_W - 1), f32) * 0.1,
        "w_short_conv": n(ks[12], (DEPTH, CONV_K, CONV_W), f32) * CONV_K ** -0.5,
        "w_branch_attn": n(ks[13], (DEPTH, ATT_W, D_MODEL), f32) * ATT_W ** -0.5,
        "w_branch_conv": n(ks[14], (DEPTH, CONV_W, D_MODEL), f32) * CONV_W ** -0.5,
        "w_out": n(ks[15], (DEPTH, D_MODEL, D_MODEL), f32) * D_MODEL ** -0.5,
        "w_up": n(ks[16], (DEPTH, D_MODEL, 2 * D_FF), f32) * D_MODEL ** -0.5,
        "w_ffn_conv": n(ks[17], (DEPTH, CONV_K, D_FF), f32) * CONV_K ** -0.5,
        "b_ffn_conv": n(ks[18], (DEPTH, D_FF), f32) * 0.01,
        "w_down": n(ks[19], (DEPTH, D_FF, D_MODEL), f32) * D_FF ** -0.5,
    }


def reference(x, c, ctx, c_ctx, w_ada, b_ada, g_pre_mix, g_post_mix, g_pre_ffn, g_post_ffn,
              w_in, rpb, w_short_conv, w_branch_attn, w_branch_conv, w_out,
              w_up, w_ffn_conv, b_ffn_conv, w_down):
    for l in range(DEPTH):
        last = l == DEPTH - 1
        mod_x = (jax.nn.silu(c) @ w_ada[l] + b_ada[l])[:, None, :]
        mod_c = jax.nn.silu(c_ctx) @ w_ada[l] + b_ada[l]
        sh1, sc1, gt1, sh2, sc2, gt2 = jnp.split(mod_x, 6, axis=-1)
        csh1, csc1, cgt1, csh2, csc2, cgt2 = jnp.split(mod_c, 6, axis=-1)

        hc = modulate(rmsnorm(ctx, g_pre_mix[l]), csh1, csc1)
        if last:
            pkv = hc @ w_in[l][:, K_OFF:BG_OFF]
            kc, vc = heads(pkv[..., :ATT_W]), heads(pkv[..., ATT_W:])
        else:
            pc = hc @ w_in[l]
            kc, vc = heads(pc[..., K_OFF:V_OFF]), heads(pc[..., V_OFF:BG_OFF])
            attn_c = context_attention(heads(pc[..., Q_OFF:K_OFF]), kc, vc)
            yc = merge_branches(pc, attn_c, w_short_conv[l], w_branch_attn[l], w_branch_conv[l], w_out[l])
            ctx_mid = ctx + cgt1 * rmsnorm(yc, g_post_mix[l])
            hc2 = modulate(rmsnorm(ctx_mid, g_pre_ffn[l]), csh2, csc2)
            ctx = ctx_mid + cgt2 * rmsnorm(conv_ffn(hc2, w_up[l], w_ffn_conv[l], b_ffn_conv[l], w_down[l]), g_post_ffn[l])

        h = modulate(rmsnorm(x, g_pre_mix[l]), sh1, sc1)
        p = h @ w_in[l]
        attn = neighbourhood_attention(heads(p[..., Q_OFF:K_OFF]), heads(p[..., K_OFF:V_OFF]),
                                       heads(p[..., V_OFF:BG_OFF]), kc, vc, rpb[l])
        y = merge_branches(p, attn, w_short_conv[l], w_branch_attn[l], w_branch_conv[l], w_out[l])
        x = x + gt1 * rmsnorm(y, g_post_mix[l])

        h = modulate(rmsnorm(x, g_pre_ffn[l]), sh2, sc2)
        x = x + gt2 * rmsnorm(conv_ffn(h, w_up[l], w_ffn_conv[l], b_ffn_conv[l], w_down[l]), g_post_ffn[l])
    return x
```

```python
import functools

import jax
import jax.numpy as jnp
from jax import lax
from jax.experimental import pallas as pl
from jax.experimental.pallas import tpu as pltpu

F32 = jnp.float32
BF16 = jnp.bfloat16

D_MODEL = 1024
BATCH = 16
SEQ = 2048
CTX_LEN = 256
GRID_W = 64
ROWS = SEQ // GRID_W
N_HEADS = 8
HEAD_DIM = 64
ATT_W = N_HEADS * HEAD_DIM
CONV_W = 512
WIN_H = 8
WIN_W = 16
D_FF = 2816
EPS = 1e-6
NEG_INF = -1e30

Q_OFF = 0
K_OFF = ATT_W
V_OFF = 2 * ATT_W
BG_OFF = 3 * ATT_W
CG_OFF = BG_OFF + CONV_W
HV_OFF = CG_OFF + CONV_W
GA_OFF = HV_OFF + CONV_W
GB_OFF = GA_OFF + D_MODEL
IN_W = GB_OFF + D_MODEL

N_MOD = 6
ADA_ROWS = 24
HALO = 16
HEAD_PAIR_W = 2 * HEAD_DIM
N_PAIRS = N_HEADS // 2
WIN_KEYS = WIN_H * GRID_W

VMEM_LIMIT_BYTES = 56 * 1024 * 1024

TM_IN = 512
TM_CTX = 512
TM_MERGE = 512
TM_FFN = 512
IN_CHUNK = 512
FF_CHUNK = 256


def _params(*sem):
    return pltpu.CompilerParams(dimension_semantics=sem, vmem_limit_bytes=VMEM_LIMIT_BYTES)


def _const_spec(shape):
    nd = len(shape)
    return pl.BlockSpec(shape, lambda *_: (0,) * nd, pipeline_mode=pl.Buffered(1))


def _dot(a, b):
    return jnp.dot(a, b, preferred_element_type=F32)


def _dot_nt(a, b):
    return lax.dot_general(a, b, (((1,), (1,)), ((), ())), preferred_element_type=F32)


def _sigmoid(x):
    return 1.0 / (1.0 + jnp.exp(-x))


def _rmsnorm(x, g):
    ms = jnp.mean(x * x, axis=-1, keepdims=True)
    return x * lax.rsqrt(ms + EPS) * g


def _gelu_tanh(x):
    c = 0.7978845608028654
    return x * (0.5 * (1.0 + jnp.tanh(c * (x + 0.044715 * (x * x * x)))))


def _ada_kernel(c_ref, w_ref, b_ref, o_ref):
    c = c_ref[...]
    a = (c * _sigmoid(c)).astype(BF16)
    o_ref[...] = _dot(a, w_ref[...].astype(BF16)) + b_ref[...]


def _ada(c_all, w_ada, b_ada):
    n = w_ada.shape[1]
    tn = 512
    return pl.pallas_call(
        _ada_kernel,
        grid=(n // tn,),
        in_specs=[
            _const_spec((ADA_ROWS, D_MODEL)),
            pl.BlockSpec((D_MODEL, tn), lambda j: (0, j)),
            pl.BlockSpec((1, tn), lambda j: (0, j)),
        ],
        out_specs=pl.BlockSpec((ADA_ROWS, tn), lambda j: (0, j)),
        out_shape=jax.ShapeDtypeStruct((ADA_ROWS, n), F32),
        compiler_params=_params("arbitrary"),
        name="ada",
    )(c_all, w_ada, b_ada)


def _bias_kernel(rpb_ref, o_ref):
    h = pl.program_id(0)
    qcol = lax.broadcasted_iota(jnp.int32, (GRID_W, GRID_W), 0)
    kcol = lax.broadcasted_iota(jnp.int32, (GRID_W, GRID_W), 1)
    cs = jnp.clip(qcol - WIN_W // 2, 0, GRID_W - WIN_W)
    valid = (kcol >= cs) & (kcol < cs + WIN_W)
    dc = jnp.clip(kcol - qcol + WIN_W - 1, 0, 2 * WIN_W - 2)
    for dr in range(2 * WIN_H - 1):
        slab = jnp.zeros((GRID_W, GRID_W), F32)
        for k in range(2 * WIN_W - 1):
            slab = jnp.where(dc == k, rpb_ref[h, dr, k], slab)
        slab = jnp.where(valid, slab, NEG_INF)
        for case in range(WIN_H):
            j = dr - (WIN_H - 1) + case
            if 0 <= j < WIN_H:
                o_ref[case, 0, :, j * GRID_W:(j + 1) * GRID_W] = slab


def _bias_table(rpb):
    return pl.pallas_call(
        _bias_kernel,
        grid=(N_HEADS,),
        in_specs=[pl.BlockSpec(memory_space=pltpu.SMEM)],
        out_specs=pl.BlockSpec((WIN_H, 1, GRID_W, WIN_KEYS), lambda h: (0, h, 0, 0)),
        out_shape=jax.ShapeDtypeStruct((WIN_H, N_HEADS, GRID_W, WIN_KEYS), F32),
        compiler_params=_params("arbitrary"),
        name="bias_table",
    )(rpb)


def _ctx_kernel(ctx_ref, g_ref, modc_ref, wk_ref, wv_ref, kc_ref, vc_ref):
    y = _rmsnorm(ctx_ref[...], g_ref[...])
    h = (y * (1.0 + modc_ref[1:2, :]) + modc_ref[0:1, :]).astype(BF16)
    kc_ref[...] = _dot(h, wk_ref[...]).astype(BF16)
    vc_ref[...] = _dot(h, wv_ref[...]).astype(BF16)


def _ctx_kv(ctx2, g_pre, mod_c, w_in_bf):
    m = ctx2.shape[0]
    return pl.pallas_call(
        _ctx_kernel,
        grid=(m // TM_CTX,),
        in_specs=[
            pl.BlockSpec((TM_CTX, D_MODEL), lambda i: (i, 0)),
            _const_spec((1, D_MODEL)),
            _const_spec((N_MOD, D_MODEL)),
            pl.BlockSpec((D_MODEL, ATT_W), lambda i: (0, K_OFF // ATT_W), pipeline_mode=pl.Buffered(1)),
            pl.BlockSpec((D_MODEL, ATT_W), lambda i: (0, V_OFF // ATT_W), pipeline_mode=pl.Buffered(1)),
        ],
        out_specs=[
            pl.BlockSpec((TM_CTX, ATT_W), lambda i: (i, 0)),
            pl.BlockSpec((TM_CTX, ATT_W), lambda i: (i, 0)),
        ],
        out_shape=[jax.ShapeDtypeStruct((m, ATT_W), BF16)] * 2,
        compiler_params=_params("arbitrary"),
        name="ctx_kv",
    )(ctx2, g_pre, mod_c, w_in_bf, w_in_bf)


def _in_kernel(x_ref, g_ref, mod_ref, w_ref, p_ref):
    y = _rmsnorm(x_ref[...], g_ref[...])
    h = (y * (1.0 + mod_ref[0, 1:2, :]) + mod_ref[0, 0:1, :]).astype(BF16)
    for n in range(0, IN_W, IN_CHUNK):
        p_ref[:, n:n + IN_CHUNK] = _dot(h, w_ref[:, n:n + IN_CHUNK]).astype(BF16)


def _in_proj(x2, g_pre, mod, w_in_bf):
    m = x2.shape[0]
    tpb = SEQ // TM_IN
    return pl.pallas_call(
        _in_kernel,
        grid=(m // TM_IN,),
        in_specs=[
            pl.BlockSpec((TM_IN, D_MODEL), lambda i: (i, 0)),
            _const_spec((1, D_MODEL)),
            pl.BlockSpec((1, N_MOD, D_MODEL), lambda i: (i // tpb, 0, 0)),
            _const_spec((D_MODEL, IN_W)),
        ],
        out_specs=pl.BlockSpec((TM_IN, IN_W), lambda i: (i, 0)),
        out_shape=jax.ShapeDtypeStruct((m, IN_W), BF16),
        compiler_params=_params("arbitrary"),
        name="in_proj",
    )(x2, g_pre, mod, w_in_bf)


def _attn_kernel(q_ref, k_ref, v_ref, kc_ref, vc_ref, bias_ref, o_ref):
    lane = lax.broadcasted_iota(jnp.int32, (GRID_W, HEAD_PAIR_W), 1)
    first_head = lane < HEAD_DIM
    zero = jnp.zeros((GRID_W, HEAD_PAIR_W), BF16)

    def row(r, carry):
        rs = jnp.clip(r - WIN_H // 2, 0, ROWS - WIN_H)
        case = r - rs
        q0 = pl.multiple_of(r * GRID_W, GRID_W)
        k0 = pl.multiple_of(rs * GRID_W, GRID_W)
        for pr in range(N_PAIRS):
            cols = slice(pr * HEAD_PAIR_W, (pr + 1) * HEAD_PAIR_W)
            qp = q_ref[pl.ds(q0, GRID_W), cols] * jnp.asarray(HEAD_DIM ** -0.5, BF16)
            qm = jnp.concatenate([jnp.where(first_head, qp, zero), jnp.where(first_head, zero, qp)], axis=0)
            s_w = _dot_nt(qm, k_ref[pl.ds(k0, WIN_KEYS), cols])
            s_c = _dot_nt(qm, kc_ref[:, cols])
            s_w = s_w + jnp.concatenate([bias_ref[case, 2 * pr], bias_ref[case, 2 * pr + 1]], axis=0)
            m = jnp.maximum(jnp.max(s_w, axis=-1, keepdims=True), jnp.max(s_c, axis=-1, keepdims=True))
            e_w = jnp.exp(s_w - m)
            e_c = jnp.exp(s_c - m)
            l = jnp.sum(e_w, axis=-1, keepdims=True) + jnp.sum(e_c, axis=-1, keepdims=True)
            o2 = _dot(e_w.astype(BF16), v_ref[pl.ds(k0, WIN_KEYS), cols]) + _dot(e_c.astype(BF16), vc_ref[:, cols])
            o2 = o2 / l
            o = jnp.where(first_head, o2[:GRID_W], o2[GRID_W:])
            o_ref[pl.ds(q0, GRID_W), cols] = o.astype(BF16)
        return carry

    lax.fori_loop(0, ROWS, row, 0)


def _attention(p, kc, vc, bias):
    seq_spec = lambda col: pl.BlockSpec((SEQ, ATT_W), lambda b: (b, col))
    ctx_spec = pl.BlockSpec((CTX_LEN, ATT_W), lambda b: (b, 0))
    return pl.pallas_call(
        _attn_kernel,
        grid=(BATCH,),
        in_specs=[
            seq_spec(Q_OFF // ATT_W), seq_spec(K_OFF // ATT_W), seq_spec(V_OFF // ATT_W),
            ctx_spec, ctx_spec,
            _const_spec((WIN_H, N_HEADS, GRID_W, WIN_KEYS)),
        ],
        out_specs=pl.BlockSpec((SEQ, ATT_W), lambda b: (b, 0)),
        out_shape=jax.ShapeDtypeStruct((BATCH * SEQ, ATT_W), BF16),
        compiler_params=_params("arbitrary"),
        name="attention",
    )(p, p, p, kc, vc, bias)


def _merge_kernel(attn_ref, bg_ref, cg_ref, hv_ref, cgp_ref, hvp_ref, cgn_ref, hvn_ref, ga_ref, gb_ref,
                  x_ref, mod_ref, gpost_ref, gpre_ref, wc_ref, wba_ref, wbb_ref, wo_ref,
                  x1_ref, h2_ref, m_scr):
    tm = TM_MERGE
    t = pl.program_id(0) % (SEQ // tm)
    has_prev = (t > 0).astype(F32)
    has_next = (t < SEQ // tm - 1).astype(F32)
    m_mid = cg_ref[...].astype(F32) * hv_ref[...].astype(F32)
    m_scr[0:HALO, :] = cgp_ref[...].astype(F32) * hvp_ref[...].astype(F32) * has_prev
    m_scr[HALO:HALO + tm, :] = m_mid
    m_scr[HALO + tm:, :] = cgn_ref[...].astype(F32) * hvn_ref[...].astype(F32) * has_next
    conv = (wc_ref[0:1, :] * m_scr[HALO - 1:HALO - 1 + tm, :] + wc_ref[1:2, :] * m_mid
            + wc_ref[2:3, :] * m_scr[HALO + 1:HALO + 1 + tm, :])
    z = (bg_ref[...].astype(F32) * conv).astype(BF16)
    y_attn = _dot(attn_ref[...], wba_ref[...])
    y_conv = _dot(z, wbb_ref[...])
    mix = _sigmoid(ga_ref[...].astype(F32)) * y_attn + _sigmoid(gb_ref[...].astype(F32)) * y_conv
    y = _dot(mix.astype(BF16), wo_ref[...])
    x1 = x_ref[...] + mod_ref[0, 2:3, :] * _rmsnorm(y, gpost_ref[...])
    x1_ref[...] = x1
    h2 = _rmsnorm(x1, gpre_ref[...]) * (1.0 + mod_ref[0, 4:5, :]) + mod_ref[0, 3:4, :]
    h2_ref[...] = h2.astype(BF16)


def _merge(attn, p, x2, mod, g_post, g_pre_ffn, w_conv, w_ba, w_bb, w_o):
    m = x2.shape[0]
    tm = TM_MERGE
    tpb = SEQ // tm
    hb = tm // HALO
    n_halo = m // HALO
    col = lambda off, w: pl.BlockSpec((tm, w), lambda i: (i, off // w))
    prev = lambda off: pl.BlockSpec((HALO, CONV_W), lambda i: (jnp.maximum(i * hb - 1, 0), off // CONV_W))
    nxt = lambda off: pl.BlockSpec((HALO, CONV_W), lambda i: (jnp.minimum((i + 1) * hb, n_halo - 1), off // CONV_W))
    return pl.pallas_call(
        _merge_kernel,
        grid=(m // tm,),
        in_specs=[
            pl.BlockSpec((tm, ATT_W), lambda i: (i, 0)),
            col(BG_OFF, CONV_W), col(CG_OFF, CONV_W), col(HV_OFF, CONV_W),
            prev(CG_OFF), prev(HV_OFF), nxt(CG_OFF), nxt(HV_OFF),
            col(GA_OFF, D_MODEL), col(GB_OFF, D_MODEL),
            pl.BlockSpec((tm, D_MODEL), lambda i: (i, 0)),
            pl.BlockSpec((1, N_MOD, D_MODEL), lambda i: (i // tpb, 0, 0)),
            _const_spec((1, D_MODEL)), _const_spec((1, D_MODEL)),
            _const_spec((3, CONV_W)),
            _const_spec((ATT_W, D_MODEL)), _const_spec((CONV_W, D_MODEL)), _const_spec((D_MODEL, D_MODEL)),
        ],
        out_specs=[
            pl.BlockSpec((tm, D_MODEL), lambda i: (i, 0)),
            pl.BlockSpec((tm, D_MODEL), lambda i: (i, 0)),
        ],
        out_shape=[jax.ShapeDtypeStruct((m, D_MODEL), F32), jax.ShapeDtypeStruct((m, D_MODEL), BF16)],
        scratch_shapes=[pltpu.VMEM((tm + 2 * HALO, CONV_W), F32)],
        compiler_params=_params("arbitrary"),
        name="merge",
    )(attn, p, p, p, p, p, p, p, p, p, x2, mod, g_post, g_pre_ffn, w_conv, w_ba, w_bb, w_o)


def _ffn_kernel(h_ref, hp_ref, hn_ref, x1_ref, mod_ref, gpost_ref, wup_ref, wc_ref, bc_ref, wdn_ref,
                o_ref, hx_scr, g_scr, act_scr):
    tm = TM_FFN
    t = pl.program_id(0) % (SEQ // tm)
    hx_scr[0:HALO, :] = jnp.where(t > 0, hp_ref[...], jnp.zeros_like(hp_ref))
    hx_scr[HALO:HALO + tm, :] = h_ref[...]
    hx_scr[HALO + tm:, :] = jnp.where(t < SEQ // tm - 1, hn_ref[...], jnp.zeros_like(hn_ref))
    for n in range(0, D_FF, FF_CHUNK):
        a = _dot(h_ref[...], wup_ref[:, n:n + FF_CHUNK])
        g_scr[...] = _dot(hx_scr[...], wup_ref[:, D_FF + n:D_FF + n + FF_CHUNK])
        conv = (wc_ref[0:1, n:n + FF_CHUNK] * g_scr[HALO - 1:HALO - 1 + tm, :]
                + wc_ref[1:2, n:n + FF_CHUNK] * g_scr[HALO:HALO + tm, :]
                + wc_ref[2:3, n:n + FF_CHUNK] * g_scr[HALO + 1:HALO + 1 + tm, :]
                + bc_ref[:, n:n + FF_CHUNK])
        act_scr[:, n:n + FF_CHUNK] = (_gelu_tanh(conv) * a).astype(BF16)
    y = _dot(act_scr[...], wdn_ref[...])
    o_ref[...] = x1_ref[...] + mod_ref[0, 5:6, :] * _rmsnorm(y, gpost_ref[...])


def _ffn(h2, x1, mod, g_post, w_up, w_conv, b_conv, w_down):
    m = x1.shape[0]
    tm = TM_FFN
    tpb = SEQ // tm
    hb = tm // HALO
    n_halo = m // HALO
    return pl.pallas_call(
        _ffn_kernel,
        grid=(m // tm,),
        in_specs=[
            pl.BlockSpec((tm, D_MODEL), lambda i: (i, 0)),
            pl.BlockSpec((HALO, D_MODEL), lambda i: (jnp.maximum(i * hb - 1, 0), 0)),
            pl.BlockSpec((HALO, D_MODEL), lambda i: (jnp.minimum((i + 1) * hb, n_halo - 1), 0)),
            pl.BlockSpec((tm, D_MODEL), lambda i: (i, 0)),
            pl.BlockSpec((1, N_MOD, D_MODEL), lambda i: (i // tpb, 0, 0)),
            _const_spec((1, D_MODEL)),
            _const_spec((D_MODEL, 2 * D_FF)),
            _const_spec((3, D_FF)),
            _const_spec((1, D_FF)),
            _const_spec((D_FF, D_MODEL)),
        ],
        out_specs=pl.BlockSpec((tm, D_MODEL), lambda i: (i, 0)),
        out_shape=jax.ShapeDtypeStruct((m, D_MODEL), F32),
        scratch_shapes=[
            pltpu.VMEM((tm + 2 * HALO, D_MODEL), BF16),
            pltpu.VMEM((tm + 2 * HALO, FF_CHUNK), F32),
            pltpu.VMEM((tm, D_FF), BF16),
        ],
        compiler_params=_params("arbitrary"),
        name="conv_ffn",
    )(h2, h2, h2, x1, mod, g_post, w_up, w_conv, b_conv, w_down)


def kernel(x, c, ctx, c_ctx, w_ada, b_ada, g_pre_mix, g_post_mix, g_pre_ffn, g_post_ffn, w_in, rpb,
           w_short_conv, w_branch_attn, w_branch_conv, w_out, w_up, w_ffn_conv, b_ffn_conv, w_down):
    depth = w_in.shape[0]
    assert depth == 1, "single-layer configuration"
    b, s, d = x.shape
    x2 = x.reshape(b * s, d)
    ctx2 = ctx.reshape(b * CTX_LEN, d)

    c_all = jnp.concatenate([c, c_ctx[None, :], jnp.zeros((ADA_ROWS - b - 1, d), F32)], axis=0)
    mod_all = _ada(c_all, w_ada[0], b_ada[0][None, :])
    mod = mod_all[:b].reshape(b, N_MOD, d)
    mod_c = mod_all[b].reshape(N_MOD, d)

    w_in_bf = w_in[0].astype(BF16)
    bias = _bias_table(rpb[0])
    kc, vc = _ctx_kv(ctx2, g_pre_mix[0][None, :], mod_c, w_in_bf)
    p = _in_proj(x2, g_pre_mix[0][None, :], mod, w_in_bf)
    attn = _attention(p, kc, vc, bias)
    x1, h2 = _merge(attn, p, x2, mod, g_post_mix[0][None, :], g_pre_ffn[0][None, :], w_short_conv[0],
                    w_branch_attn[0].astype(BF16), w_branch_conv[0].astype(BF16), w_out[0].astype(BF16))
    out = _ffn(h2, x1, mod, g_post_ffn[0][None, :], w_up[0].astype(BF16), w_ffn_conv[0],
               b_ffn_conv[0][None, :], w_down[0].astype(BF16))
    return out.reshape(b, s, d)
```

```python
import functools

import jax
import jax.numpy as jnp
from jax import lax
from jax.experimental import pallas as pl
from jax.experimental.pallas import tpu as pltpu

F32 = jnp.float32
BF16 = jnp.bfloat16

D_MODEL = 1024
BATCH = 16
SEQ = 2048
CTX_LEN = 256
GRID_W = 64
ROWS = SEQ // GRID_W
N_HEADS = 8
HEAD_DIM = 64
ATT_W = N_HEADS * HEAD_DIM
CONV_W = 512
WIN_H = 8
WIN_W = 16
D_FF = 2816
EPS = 1e-6
NEG_INF = -1e30

Q_OFF = 0
K_OFF = ATT_W
V_OFF = 2 * ATT_W
BG_OFF = 3 * ATT_W
CG_OFF = BG_OFF + CONV_W
HV_OFF = CG_OFF + CONV_W
GA_OFF = HV_OFF + CONV_W
GB_OFF = GA_OFF + D_MODEL
IN_W = GB_OFF + D_MODEL

P_Q = 0
P_K = ATT_W
P_V = 2 * ATT_W
P_BG = 3 * ATT_W
P_SA = P_BG + CONV_W
P_SB = P_SA + D_MODEL
P_M = P_SB + D_MODEL
P_W = P_M + CONV_W

N_MOD = 6
ADA_ROWS = 24
HALO = 16
HEAD_PAIR_W = 2 * HEAD_DIM
N_PAIRS = N_HEADS // 2
WIN_KEYS = WIN_H * GRID_W
ALL_KEYS = WIN_KEYS + CTX_LEN
LOG2_E = 1.4426950408889634
QK_SCALE = HEAD_DIM ** -0.5 * LOG2_E

VMEM_LIMIT_BYTES = 56 * 1024 * 1024

TM_IN = 1024
TM_CTX = 512
TM_MERGE = 1024
TM_FFN = 1024
IN_CHUNK = 512
FF_CHUNK = 256
SUB_ROWS = 256
ATTN_ROWS_PER_STEP = 8
ATTN_LOOKAHEAD = 4


def _params(*sem):
    return pltpu.CompilerParams(dimension_semantics=sem, vmem_limit_bytes=VMEM_LIMIT_BYTES)


def _const_spec(shape):
    nd = len(shape)
    return pl.BlockSpec(shape, lambda *_: (0,) * nd, pipeline_mode=pl.Buffered(1))


def _dot(a, b):
    return jnp.dot(a, b, preferred_element_type=F32)


def _dot_nt(a, b):
    return lax.dot_general(a, b, (((1,), (1,)), ((), ())), preferred_element_type=F32)


def _sigmoid(x):
    return 1.0 / (1.0 + jnp.exp(-x))


def _rmsnorm(x, g):
    ms = jnp.mean(x * x, axis=-1, keepdims=True)
    return x * lax.rsqrt(ms + EPS) * g


def _gelu_tanh(x):
    c = 0.7978845608028654
    return x * (0.5 * (1.0 + jnp.tanh(c * (x + 0.044715 * (x * x * x)))))


def _ada_kernel(c_ref, w_ref, b_ref, o_ref):
    c = c_ref[...]
    a = (c * _sigmoid(c)).astype(BF16)
    o_ref[...] = _dot(a, w_ref[...].astype(BF16)) + b_ref[...]


def _ada(c_all, w_ada, b_ada):
    n = w_ada.shape[1]
    tn = 512
    return pl.pallas_call(
        _ada_kernel,
        grid=(n // tn,),
        in_specs=[
            _const_spec((ADA_ROWS, D_MODEL)),
            pl.BlockSpec((D_MODEL, tn), lambda j: (0, j)),
            pl.BlockSpec((1, tn), lambda j: (0, j)),
        ],
        out_specs=pl.BlockSpec((ADA_ROWS, tn), lambda j: (0, j)),
        out_shape=jax.ShapeDtypeStruct((ADA_ROWS, n), F32),
        compiler_params=_params("arbitrary"),
        name="ada",
    )(c_all, w_ada, b_ada)


def _bias_kernel(rpb_ref, o_ref):
    h = pl.program_id(0)
    qcol = lax.broadcasted_iota(jnp.int32, (GRID_W, GRID_W), 0)
    kcol = lax.broadcasted_iota(jnp.int32, (GRID_W, GRID_W), 1)
    cs = jnp.clip(qcol - WIN_W // 2, 0, GRID_W - WIN_W)
    valid = (kcol >= cs) & (kcol < cs + WIN_W)
    dc = jnp.clip(kcol - qcol + WIN_W - 1, 0, 2 * WIN_W - 2)
    for dr in range(2 * WIN_H - 1):
        slab = jnp.zeros((GRID_W, GRID_W), F32)
        for k in range(2 * WIN_W - 1):
            slab = jnp.where(dc == k, rpb_ref[h, dr, k], slab)
        slab = jnp.where(valid, slab * LOG2_E, NEG_INF)
        for case in range(WIN_H):
            j = dr - (WIN_H - 1) + case
            if 0 <= j < WIN_H:
                o_ref[case, 0, :, j * GRID_W:(j + 1) * GRID_W] = slab


def _bias_table(rpb):
    return pl.pallas_call(
        _bias_kernel,
        grid=(N_HEADS,),
        in_specs=[pl.BlockSpec(memory_space=pltpu.SMEM)],
        out_specs=pl.BlockSpec((WIN_H, 1, GRID_W, WIN_KEYS), lambda h: (0, h, 0, 0)),
        out_shape=jax.ShapeDtypeStruct((WIN_H, N_HEADS, GRID_W, WIN_KEYS), F32),
        compiler_params=_params("arbitrary"),
        name="bias_table",
    )(rpb)


def _ctx_kernel(ctx_ref, g_ref, modc_ref, wk_ref, wv_ref, kc_ref, vc_ref):
    y = _rmsnorm(ctx_ref[...], g_ref[...])
    h = (y * (1.0 + modc_ref[1:2, :]) + modc_ref[0:1, :]).astype(BF16)
    kc_ref[...] = _dot(h, wk_ref[...]).astype(BF16)
    vc_ref[...] = _dot(h, wv_ref[...]).astype(BF16)


def _ctx_kv(ctx2, g_pre, mod_c, w_in_bf):
    m = ctx2.shape[0]
    return pl.pallas_call(
        _ctx_kernel,
        grid=(m // TM_CTX,),
        in_specs=[
            pl.BlockSpec((TM_CTX, D_MODEL), lambda i: (i, 0)),
            _const_spec((1, D_MODEL)),
            _const_spec((N_MOD, D_MODEL)),
            pl.BlockSpec((D_MODEL, ATT_W), lambda i: (0, K_OFF // ATT_W), pipeline_mode=pl.Buffered(1)),
            pl.BlockSpec((D_MODEL, ATT_W), lambda i: (0, V_OFF // ATT_W), pipeline_mode=pl.Buffered(1)),
        ],
        out_specs=[
            pl.BlockSpec((TM_CTX, ATT_W), lambda i: (i, 0)),
            pl.BlockSpec((TM_CTX, ATT_W), lambda i: (i, 0)),
        ],
        out_shape=[jax.ShapeDtypeStruct((m, ATT_W), BF16)] * 2,
        compiler_params=_params("arbitrary"),
        name="ctx_kv",
    )(ctx2, g_pre, mod_c, w_in_bf, w_in_bf)


def _in_kernel(x_ref, g_ref, mod_ref, w_ref, p_ref):
    y = _rmsnorm(x_ref[...], g_ref[...])
    h = (y * (1.0 + mod_ref[0, 1:2, :]) + mod_ref[0, 0:1, :]).astype(BF16)
    def proj(off, width=IN_CHUNK):
        return _dot(h, w_ref[:, off:off + width])

    def put(off, val):
        p_ref[:, off:off + val.shape[1]] = val.astype(BF16)

    for n in range(0, D_MODEL, IN_CHUNK):
        put(P_SA + n, _sigmoid(proj(GA_OFF + n)))
        put(P_SB + n, _sigmoid(proj(GB_OFF + n)))
    for n in range(0, CONV_W, IN_CHUNK):
        put(P_M + n, proj(CG_OFF + n) * proj(HV_OFF + n))
        put(P_BG + n, proj(BG_OFF + n))
    for n in range(0, ATT_W, IN_CHUNK):
        put(P_Q + n, proj(Q_OFF + n) * QK_SCALE)
        put(P_K + n, proj(K_OFF + n))
        put(P_V + n, proj(V_OFF + n))


def _in_proj(x2, g_pre, mod, w_in_bf):
    m = x2.shape[0]
    tpb = SEQ // TM_IN
    return pl.pallas_call(
        _in_kernel,
        grid=(m // TM_IN,),
        in_specs=[
            pl.BlockSpec((TM_IN, D_MODEL), lambda i: (i, 0)),
            _const_spec((1, D_MODEL)),
            pl.BlockSpec((1, N_MOD, D_MODEL), lambda i: (i // tpb, 0, 0)),
            _const_spec((D_MODEL, IN_W)),
        ],
        out_specs=pl.BlockSpec((TM_IN, P_W), lambda i: (i, 0)),
        out_shape=jax.ShapeDtypeStruct((m, P_W), BF16),
        compiler_params=_params("arbitrary"),
        name="in_proj",
    )(x2, g_pre, mod, w_in_bf)


def _attn_kernel(q_ref, k_ref, v_ref, kc_ref, vc_ref, bias_ref, o_ref):
    lane = lax.broadcasted_iota(jnp.int32, (GRID_W, HEAD_PAIR_W), 1)
    first_head = lane < HEAD_DIM
    zero = jnp.zeros((GRID_W, HEAD_PAIR_W), BF16)

    def offsets(r):
        rs = jnp.clip(r - WIN_H // 2, 0, ROWS - WIN_H)
        return r - rs, pl.multiple_of(r * GRID_W, GRID_W), pl.multiple_of(rs * GRID_W, GRID_W)

    def scores(r, pr):
        case, q0, k0 = offsets(r)
        cols = slice(pr * HEAD_PAIR_W, (pr + 1) * HEAD_PAIR_W)
        qp = q_ref[pl.ds(q0, GRID_W), cols]
        qm = jnp.concatenate([jnp.where(first_head, qp, zero), jnp.where(first_head, zero, qp)], axis=0)
        keys = jnp.concatenate([k_ref[pl.ds(k0, WIN_KEYS), cols], kc_ref[:, cols]], axis=0)
        s = _dot_nt(qm, keys)
        bias = jnp.concatenate([bias_ref[case, 2 * pr], bias_ref[case, 2 * pr + 1]], axis=0)
        return jnp.concatenate([s[:, :WIN_KEYS] + bias, s[:, WIN_KEYS:]], axis=1)

    def outputs(r, pr, s):
        _, q0, k0 = offsets(r)
        cols = slice(pr * HEAD_PAIR_W, (pr + 1) * HEAD_PAIR_W)
        e = jnp.exp2(s - jnp.max(s, axis=-1, keepdims=True))
        l = jnp.sum(e, axis=-1, keepdims=True)
        vals = jnp.concatenate([v_ref[pl.ds(k0, WIN_KEYS), cols], vc_ref[:, cols]], axis=0)
        o2 = _dot(e.astype(BF16), vals) / l
        o = jnp.where(first_head, o2[:GRID_W], o2[GRID_W:])
        o_ref[pl.ds(q0, GRID_W), cols] = o.astype(BF16)

    def rows(i, carry):
        items = [(i * ATTN_ROWS_PER_STEP + dr, pr) for dr in range(ATTN_ROWS_PER_STEP) for pr in range(N_PAIRS)]
        pending = [scores(*it) for it in items[:ATTN_LOOKAHEAD]]
        for n, it in enumerate(items):
            if n + ATTN_LOOKAHEAD < len(items):
                pending.append(scores(*items[n + ATTN_LOOKAHEAD]))
            outputs(*it, pending.pop(0))
        return carry

    lax.fori_loop(0, ROWS // ATTN_ROWS_PER_STEP, rows, 0)


def _attention(p, kc, vc, bias):
    seq_spec = lambda col: pl.BlockSpec((SEQ, ATT_W), lambda b: (b, col))
    ctx_spec = pl.BlockSpec((CTX_LEN, ATT_W), lambda b: (b, 0))
    return pl.pallas_call(
        _attn_kernel,
        grid=(BATCH,),
        in_specs=[
            seq_spec(P_Q // ATT_W), seq_spec(P_K // ATT_W), seq_spec(P_V // ATT_W),
            ctx_spec, ctx_spec,
            _const_spec((WIN_H, N_HEADS, GRID_W, WIN_KEYS)),
        ],
        out_specs=pl.BlockSpec((SEQ, ATT_W), lambda b: (b, 0)),
        out_shape=jax.ShapeDtypeStruct((BATCH * SEQ, ATT_W), BF16),
        compiler_params=_params("arbitrary"),
        name="attention",
    )(p, p, p, kc, vc, bias)


def _merge_kernel(attn_ref, bg_ref, m_ref, mp_ref, mn_ref, sa_ref, sb_ref,
                  x_ref, mod_ref, gpost_ref, gpre_ref, wc_ref, wba_ref, wbb_ref, wo_ref,
                  x1_ref, h2_ref, m_scr):
    tm = TM_MERGE
    t = pl.program_id(0) % (SEQ // tm)
    has_prev = (t > 0).astype(F32)
    has_next = (t < SEQ // tm - 1).astype(F32)
    m_scr[0:HALO, :] = mp_ref[...].astype(F32) * has_prev
    m_scr[HALO:HALO + tm, :] = m_ref[...].astype(F32)
    m_scr[HALO + tm:, :] = mn_ref[...].astype(F32) * has_next
    g_mix = gpost_ref[...] * mod_ref[0, 2:3, :]
    g_ffn = gpre_ref[...] * (1.0 + mod_ref[0, 4:5, :])
    for r in range(0, tm, SUB_ROWS):
        rows = slice(r, r + SUB_ROWS)
        conv = (wc_ref[0:1, :] * m_scr[HALO - 1 + r:HALO - 1 + r + SUB_ROWS, :]
                + wc_ref[1:2, :] * m_scr[HALO + r:HALO + r + SUB_ROWS, :]
                + wc_ref[2:3, :] * m_scr[HALO + 1 + r:HALO + 1 + r + SUB_ROWS, :])
        z = (bg_ref[rows, :].astype(F32) * conv).astype(BF16)
        y_attn = _dot(attn_ref[rows, :], wba_ref[...])
        y_conv = _dot(z, wbb_ref[...])
        mix = sa_ref[rows, :].astype(F32) * y_attn + sb_ref[rows, :].astype(F32) * y_conv
        y = _dot(mix.astype(BF16), wo_ref[...])
        x1 = x_ref[rows, :] + _rmsnorm(y, g_mix)
        x1_ref[rows, :] = x1
        h2_ref[rows, :] = (_rmsnorm(x1, g_ffn) + mod_ref[0, 3:4, :]).astype(BF16)


def _merge(attn, p, x2, mod, g_post, g_pre_ffn, w_conv, w_ba, w_bb, w_o):
    m = x2.shape[0]
    tm = TM_MERGE
    tpb = SEQ // tm
    hb = tm // HALO
    n_halo = m // HALO
    col = lambda off, w: pl.BlockSpec((tm, w), lambda i: (i, off // w))
    prev = pl.BlockSpec((HALO, CONV_W), lambda i: (jnp.maximum(i * hb - 1, 0), P_M // CONV_W))
    nxt = pl.BlockSpec((HALO, CONV_W), lambda i: (jnp.minimum((i + 1) * hb, n_halo - 1), P_M // CONV_W))
    return pl.pallas_call(
        _merge_kernel,
        grid=(m // tm,),
        in_specs=[
            pl.BlockSpec((tm, ATT_W), lambda i: (i, 0)),
            col(P_BG, CONV_W), col(P_M, CONV_W), prev, nxt,
            col(P_SA, D_MODEL), col(P_SB, D_MODEL),
            pl.BlockSpec((tm, D_MODEL), lambda i: (i, 0)),
            pl.BlockSpec((1, N_MOD, D_MODEL), lambda i: (i // tpb, 0, 0)),
            _const_spec((1, D_MODEL)), _const_spec((1, D_MODEL)),
            _const_spec((3, CONV_W)),
            _const_spec((ATT_W, D_MODEL)), _const_spec((CONV_W, D_MODEL)), _const_spec((D_MODEL, D_MODEL)),
        ],
        out_specs=[
            pl.BlockSpec((tm, D_MODEL), lambda i: (i, 0)),
            pl.BlockSpec((tm, D_MODEL), lambda i: (i, 0)),
        ],
        out_shape=[jax.ShapeDtypeStruct((m, D_MODEL), F32), jax.ShapeDtypeStruct((m, D_MODEL), BF16)],
        scratch_shapes=[pltpu.VMEM((tm + 2 * HALO, CONV_W), F32)],
        compiler_params=_params("arbitrary"),
        name="merge",
    )(attn, p, p, p, p, p, p, x2, mod, g_post, g_pre_ffn, w_conv, w_ba, w_bb, w_o)


def _ffn_kernel(h_ref, hp_ref, hn_ref, x1_ref, mod_ref, gpost_ref, wup_ref, wc_ref, bc_ref, wdn_ref,
                o_ref, hx_scr, g_scr, act_scr):
    tm = TM_FFN
    t = pl.program_id(0) % (SEQ // tm)
    hx_scr[0:HALO, :] = jnp.where(t > 0, hp_ref[...], jnp.zeros_like(hp_ref))
    hx_scr[HALO:HALO + tm, :] = h_ref[...]
    hx_scr[HALO + tm:, :] = jnp.where(t < SEQ // tm - 1, hn_ref[...], jnp.zeros_like(hn_ref))
    for c, n in enumerate(range(0, D_FF, FF_CHUNK)):
        g = g_scr.at[c % 2]
        a = _dot(h_ref[...], wup_ref[:, n:n + FF_CHUNK])
        g[...] = _dot(hx_scr[...], wup_ref[:, D_FF + n:D_FF + n + FF_CHUNK])
        conv = (wc_ref[0:1, n:n + FF_CHUNK] * g[HALO - 1:HALO - 1 + tm, :]
                + wc_ref[1:2, n:n + FF_CHUNK] * g[HALO:HALO + tm, :]
                + wc_ref[2:3, n:n + FF_CHUNK] * g[HALO + 1:HALO + 1 + tm, :]
                + bc_ref[:, n:n + FF_CHUNK])
        act_scr[:, n:n + FF_CHUNK] = (_gelu_tanh(conv) * a).astype(BF16)
    g_out = gpost_ref[...] * mod_ref[0, 5:6, :]
    for r in range(0, tm, SUB_ROWS):
        rows = slice(r, r + SUB_ROWS)
        y = _dot(act_scr[rows, :], wdn_ref[...])
        o_ref[rows, :] = x1_ref[rows, :] + _rmsnorm(y, g_out)


def _ffn(h2, x1, mod, g_post, w_up, w_conv, b_conv, w_down):
    m = x1.shape[0]
    tm = TM_FFN
    tpb = SEQ // tm
    hb = tm // HALO
    n_halo = m // HALO
    return pl.pallas_call(
        _ffn_kernel,
        grid=(m // tm,),
        in_specs=[
            pl.BlockSpec((tm, D_MODEL), lambda i: (i, 0)),
            pl.BlockSpec((HALO, D_MODEL), lambda i: (jnp.maximum(i * hb - 1, 0), 0)),
            pl.BlockSpec((HALO, D_MODEL), lambda i: (jnp.minimum((i + 1) * hb, n_halo - 1), 0)),
            pl.BlockSpec((tm, D_MODEL), lambda i: (i, 0)),
            pl.BlockSpec((1, N_MOD, D_MODEL), lambda i: (i // tpb, 0, 0)),
            _const_spec((1, D_MODEL)),
            _const_spec((D_MODEL, 2 * D_FF)),
            _const_spec((3, D_FF)),
            _const_spec((1, D_FF)),
            _const_spec((D_FF, D_MODEL)),
        ],
        out_specs=pl.BlockSpec((tm, D_MODEL), lambda i: (i, 0)),
        out_shape=jax.ShapeDtypeStruct((m, D_MODEL), F32),
        scratch_shapes=[
            pltpu.VMEM((tm + 2 * HALO, D_MODEL), BF16),
            pltpu.VMEM((2, tm + 2 * HALO, FF_CHUNK), F32),
            pltpu.VMEM((tm, D_FF), BF16),
        ],
        compiler_params=_params("arbitrary"),
        name="conv_ffn",
    )(h2, h2, h2, x1, mod, g_post, w_up, w_conv, b_conv, w_down)


def kernel(x, c, ctx, c_ctx, w_ada, b_ada, g_pre_mix, g_post_mix, g_pre_ffn, g_post_ffn, w_in, rpb,
           w_short_conv, w_branch_attn, w_branch_conv, w_out, w_up, w_ffn_conv, b_ffn_conv, w_down):
    depth = w_in.shape[0]
    assert depth == 1, "single-layer configuration"
    b, s, d = x.shape
    x2 = x.reshape(b * s, d)
    ctx2 = ctx.reshape(b * CTX_LEN, d)

    c_all = jnp.concatenate([c, c_ctx[None, :], jnp.zeros((ADA_ROWS - b - 1, d), F32)], axis=0)
    mod_all = _ada(c_all, w_ada[0], b_ada[0][None, :])
    mod = mod_all[:b].reshape(b, N_MOD, d)
    mod_c = mod_all[b].reshape(N_MOD, d)

    w_in_bf = w_in[0].astype(BF16)
    bias = _bias_table(rpb[0])
    kc, vc = _ctx_kv(ctx2, g_pre_mix[0][None, :], mod_c, w_in_bf)
    p = _in_proj(x2, g_pre_mix[0][None, :], mod, w_in_bf)
    attn = _attention(p, kc, vc, bias)
    x1, h2 = _merge(attn, p, x2, mod, g_post_mix[0][None, :], g_pre_ffn[0][None, :], w_short_conv[0],
                    w_branch_attn[0].astype(BF16), w_branch_conv[0].astype(BF16), w_out[0].astype(BF16))
    out = _ffn(h2, x1, mod, g_post_ffn[0][None, :], w_up[0].astype(BF16), w_ffn_conv[0],
               b_ffn_conv[0][None, :], w_down[0].astype(BF16))
    return out.reshape(b, s, d)
```

```python
import functools

import jax
import jax.numpy as jnp
from jax import lax
from jax.experimental import pallas as pl
from jax.experimental.pallas import tpu as pltpu

F32 = jnp.float32
BF16 = jnp.bfloat16

D_MODEL = 1024
BATCH = 16
SEQ = 2048
CTX_LEN = 256
GRID_W = 64
ROWS = SEQ // GRID_W
N_HEADS = 8
HEAD_DIM = 64
ATT_W = N_HEADS * HEAD_DIM
CONV_W = 512
WIN_H = 8
WIN_W = 16
D_FF = 2816
EPS = 1e-6
NEG_INF = -1e30

Q_OFF = 0
K_OFF = ATT_W
V_OFF = 2 * ATT_W
BG_OFF = 3 * ATT_W
CG_OFF = BG_OFF + CONV_W
HV_OFF = CG_OFF + CONV_W
GA_OFF = HV_OFF + CONV_W
GB_OFF = GA_OFF + D_MODEL
IN_W = GB_OFF + D_MODEL

P_Q = 0
P_K = ATT_W
P_V = 2 * ATT_W
P_BG = 3 * ATT_W
P_SA = P_BG + CONV_W
P_SB = P_SA + D_MODEL
P_M = P_SB + D_MODEL
P_W = P_M + CONV_W

N_MOD = 6
ADA_ROWS = 24
HALO = 16
HEAD_PAIR_W = 2 * HEAD_DIM
N_PAIRS = N_HEADS // 2
WIN_KEYS = WIN_H * GRID_W
ALL_KEYS = WIN_KEYS + CTX_LEN
LOG2_E = 1.4426950408889634
QK_SCALE = HEAD_DIM ** -0.5 * LOG2_E

VMEM_LIMIT_BYTES = 56 * 1024 * 1024

ADA_TN = 512
TM_IN = 1024
TM_CTX = 512
TM_MERGE = 1024
TM_FFN = 1024
IN_CHUNK = 512
FF_CHUNK = 256
SUB_ROWS = 256
ATTN_ROWS_PER_STEP = 8
ATTN_LOOKAHEAD = 4


def _params(*sem):
    return pltpu.CompilerParams(dimension_semantics=sem, vmem_limit_bytes=VMEM_LIMIT_BYTES)


def _const_spec(shape):
    nd = len(shape)
    return pl.BlockSpec(shape, lambda *_: (0,) * nd, pipeline_mode=pl.Buffered(1))


def _dot(a, b):
    return jnp.dot(a, b, preferred_element_type=F32)


def _dot_nt(a, b):
    return lax.dot_general(a, b, (((1,), (1,)), ((), ())), preferred_element_type=F32)


def _sigmoid(x):
    return 1.0 / (1.0 + jnp.exp(-x))


def _rmsnorm(x, g):
    ms = jnp.mean(x * x, axis=-1, keepdims=True)
    return x * lax.rsqrt(ms + EPS) * g


def _gelu_tanh(x):
    c = 0.7978845608028654
    return x * (0.5 * (1.0 + jnp.tanh(c * (x + 0.044715 * (x * x * x)))))


def _write_bias(rpb_ref, h, o_ref):
    qcol = lax.broadcasted_iota(jnp.int32, (GRID_W, GRID_W), 0)
    kcol = lax.broadcasted_iota(jnp.int32, (GRID_W, GRID_W), 1)
    cs = jnp.clip(qcol - WIN_W // 2, 0, GRID_W - WIN_W)
    valid = (kcol >= cs) & (kcol < cs + WIN_W)
    dc = jnp.clip(kcol - qcol + WIN_W - 1, 0, 2 * WIN_W - 2)
    for dr in range(2 * WIN_H - 1):
        slab = jnp.zeros((GRID_W, GRID_W), F32)
        for k in range(2 * WIN_W - 1):
            slab = jnp.where(dc == k, rpb_ref[h, dr, k], slab)
        slab = jnp.where(valid, slab * LOG2_E, NEG_INF)
        for case in range(WIN_H):
            j = dr - (WIN_H - 1) + case
            if 0 <= j < WIN_H:
                o_ref[case, 0, :, j * GRID_W:(j + 1) * GRID_W] = slab


def _ada_kernel(c_ref, w_ref, b_ref, rpb_ref, o_ref, bias_ref):
    j = pl.program_id(0)
    c = c_ref[...]
    a = (c * _sigmoid(c)).astype(BF16)
    o_ref[...] = _dot(a, w_ref[...].astype(BF16)) + b_ref[...]

    @pl.when(j < N_HEADS)
    def _():
        _write_bias(rpb_ref, j, bias_ref)


def _ada_and_bias(c_all, w_ada, b_ada, rpb):
    n = w_ada.shape[1]
    assert n // ADA_TN >= N_HEADS
    return pl.pallas_call(
        _ada_kernel,
        grid=(n // ADA_TN,),
        in_specs=[
            _const_spec((ADA_ROWS, D_MODEL)),
            pl.BlockSpec((D_MODEL, ADA_TN), lambda j: (0, j)),
            pl.BlockSpec((1, ADA_TN), lambda j: (0, j)),
            pl.BlockSpec(memory_space=pltpu.SMEM),
        ],
        out_specs=[
            pl.BlockSpec((ADA_ROWS, ADA_TN), lambda j: (0, j)),
            pl.BlockSpec((WIN_H, 1, GRID_W, WIN_KEYS), lambda j: (0, jnp.minimum(j, N_HEADS - 1), 0, 0)),
        ],
        out_shape=[
            jax.ShapeDtypeStruct((ADA_ROWS, n), F32),
            jax.ShapeDtypeStruct((WIN_H, N_HEADS, GRID_W, WIN_KEYS), F32),
        ],
        compiler_params=_params("arbitrary"),
        name="ada_bias",
    )(c_all, w_ada, b_ada, rpb)


def _ctx_kernel(ctx_ref, g_ref, modc_ref, wk_ref, wv_ref, kc_ref, vc_ref):
    y = _rmsnorm(ctx_ref[...], g_ref[...])
    h = (y * (1.0 + modc_ref[1:2, :]) + modc_ref[0:1, :]).astype(BF16)
    kc_ref[...] = _dot(h, wk_ref[...]).astype(BF16)
    vc_ref[...] = _dot(h, wv_ref[...]).astype(BF16)


def _ctx_kv(ctx2, g_pre, mod_c, w_in_bf):
    m = ctx2.shape[0]
    return pl.pallas_call(
        _ctx_kernel,
        grid=(m // TM_CTX,),
        in_specs=[
            pl.BlockSpec((TM_CTX, D_MODEL), lambda i: (i, 0)),
            _const_spec((1, D_MODEL)),
            _const_spec((N_MOD, D_MODEL)),
            pl.BlockSpec((D_MODEL, ATT_W), lambda i: (0, K_OFF // ATT_W), pipeline_mode=pl.Buffered(1)),
            pl.BlockSpec((D_MODEL, ATT_W), lambda i: (0, V_OFF // ATT_W), pipeline_mode=pl.Buffered(1)),
        ],
        out_specs=[
            pl.BlockSpec((TM_CTX, ATT_W), lambda i: (i, 0)),
            pl.BlockSpec((TM_CTX, ATT_W), lambda i: (i, 0)),
        ],
        out_shape=[jax.ShapeDtypeStruct((m, ATT_W), BF16)] * 2,
        compiler_params=_params("arbitrary"),
        name="ctx_kv",
    )(ctx2, g_pre, mod_c, w_in_bf, w_in_bf)


def _in_kernel(x_ref, g_ref, mod_ref, w_ref, p_ref):
    y = _rmsnorm(x_ref[...], g_ref[...])
    h = (y * (1.0 + mod_ref[0, 1:2, :]) + mod_ref[0, 0:1, :]).astype(BF16)
    def proj(off, width=IN_CHUNK):
        return _dot(h, w_ref[:, off:off + width])

    def put(off, val):
        p_ref[:, off:off + val.shape[1]] = val.astype(BF16)

    for n in range(0, D_MODEL, IN_CHUNK):
        put(P_SA + n, _sigmoid(proj(GA_OFF + n)))
        put(P_SB + n, _sigmoid(proj(GB_OFF + n)))
    for n in range(0, CONV_W, IN_CHUNK):
        put(P_M + n, proj(CG_OFF + n) * proj(HV_OFF + n))
        put(P_BG + n, proj(BG_OFF + n))
    for n in range(0, ATT_W, IN_CHUNK):
        put(P_Q + n, proj(Q_OFF + n) * QK_SCALE)
        put(P_K + n, proj(K_OFF + n))
        put(P_V + n, proj(V_OFF + n))


def _in_proj(x2, g_pre, mod, w_in_bf):
    m = x2.shape[0]
    tpb = SEQ // TM_IN
    return pl.pallas_call(
        _in_kernel,
        grid=(m // TM_IN,),
        in_specs=[
            pl.BlockSpec((TM_IN, D_MODEL), lambda i: (i, 0)),
            _const_spec((1, D_MODEL)),
            pl.BlockSpec((1, N_MOD, D_MODEL), lambda i: (i // tpb, 0, 0)),
            _const_spec((D_MODEL, IN_W)),
        ],
        out_specs=pl.BlockSpec((TM_IN, P_W), lambda i: (i, 0)),
        out_shape=jax.ShapeDtypeStruct((m, P_W), BF16),
        compiler_params=_params("arbitrary"),
        name="in_proj",
    )(x2, g_pre, mod, w_in_bf)


def _attn_kernel(q_ref, k_ref, v_ref, kc_ref, vc_ref, bias_ref, o_ref):
    lane = lax.broadcasted_iota(jnp.int32, (GRID_W, HEAD_PAIR_W), 1)
    first_head = lane < HEAD_DIM
    zero = jnp.zeros((GRID_W, HEAD_PAIR_W), BF16)

    def offsets(r):
        rs = jnp.clip(r - WIN_H // 2, 0, ROWS - WIN_H)
        return r - rs, pl.multiple_of(r * GRID_W, GRID_W), pl.multiple_of(rs * GRID_W, GRID_W)

    def scores(r, pr):
        case, q0, k0 = offsets(r)
        cols = slice(pr * HEAD_PAIR_W, (pr + 1) * HEAD_PAIR_W)
        qp = q_ref[pl.ds(q0, GRID_W), cols]
        qm = jnp.concatenate([jnp.where(first_head, qp, zero), jnp.where(first_head, zero, qp)], axis=0)
        keys = jnp.concatenate([k_ref[pl.ds(k0, WIN_KEYS), cols], kc_ref[:, cols]], axis=0)
        s = _dot_nt(qm, keys)
        bias = jnp.concatenate([bias_ref[case, 2 * pr], bias_ref[case, 2 * pr + 1]], axis=0)
        return jnp.concatenate([s[:, :WIN_KEYS] + bias, s[:, WIN_KEYS:]], axis=1)

    def outputs(r, pr, s):
        _, q0, k0 = offsets(r)
        cols = slice(pr * HEAD_PAIR_W, (pr + 1) * HEAD_PAIR_W)
        e = jnp.exp2(s - jnp.max(s, axis=-1, keepdims=True))
        l = jnp.sum(e, axis=-1, keepdims=True)
        vals = jnp.concatenate([v_ref[pl.ds(k0, WIN_KEYS), cols], vc_ref[:, cols]], axis=0)
        o2 = _dot(e.astype(BF16), vals) / l
        o = jnp.where(first_head, o2[:GRID_W], o2[GRID_W:])
        o_ref[pl.ds(q0, GRID_W), cols] = o.astype(BF16)

    def rows(i, carry):
        items = [(i * ATTN_ROWS_PER_STEP + dr, pr) for dr in range(ATTN_ROWS_PER_STEP) for pr in range(N_PAIRS)]
        pending = [scores(*it) for it in items[:ATTN_LOOKAHEAD]]
        for n, it in enumerate(items):
            if n + ATTN_LOOKAHEAD < len(items):
                pending.append(scores(*items[n + ATTN_LOOKAHEAD]))
            outputs(*it, pending.pop(0))
        return carry

    lax.fori_loop(0, ROWS // ATTN_ROWS_PER_STEP, rows, 0)


def _attention(p, kc, vc, bias):
    seq_spec = lambda col: pl.BlockSpec((SEQ, ATT_W), lambda b: (b, col))
    ctx_spec = pl.BlockSpec((CTX_LEN, ATT_W), lambda b: (b, 0))
    return pl.pallas_call(
        _attn_kernel,
        grid=(BATCH,),
        in_specs=[
            seq_spec(P_Q // ATT_W), seq_spec(P_K // ATT_W), seq_spec(P_V // ATT_W),
            ctx_spec, ctx_spec,
            _const_spec((WIN_H, N_HEADS, GRID_W, WIN_KEYS)),
        ],
        out_specs=pl.BlockSpec((SEQ, ATT_W), lambda b: (b, 0)),
        out_shape=jax.ShapeDtypeStruct((BATCH * SEQ, ATT_W), BF16),
        compiler_params=_params("arbitrary"),
        name="attention",
    )(p, p, p, kc, vc, bias)


def _merge_kernel(attn_ref, bg_ref, m_ref, mp_ref, mn_ref, sa_ref, sb_ref,
                  x_ref, mod_ref, gpost_ref, wc_ref, wba_ref, wbb_ref, wo_ref,
                  x1_ref, m_scr):
    tm = TM_MERGE
    t = pl.program_id(0) % (SEQ // tm)
    has_prev = (t > 0).astype(F32)
    has_next = (t < SEQ // tm - 1).astype(F32)
    m_scr[0:HALO, :] = mp_ref[...].astype(F32) * has_prev
    m_scr[HALO:HALO + tm, :] = m_ref[...].astype(F32)
    m_scr[HALO + tm:, :] = mn_ref[...].astype(F32) * has_next
    g_mix = gpost_ref[...] * mod_ref[0, 2:3, :]

    def branches(r):
        rows = slice(r, r + SUB_ROWS)
        conv = (wc_ref[0:1, :] * m_scr[HALO - 1 + r:HALO - 1 + r + SUB_ROWS, :]
                + wc_ref[1:2, :] * m_scr[HALO + r:HALO + r + SUB_ROWS, :]
                + wc_ref[2:3, :] * m_scr[HALO + 1 + r:HALO + 1 + r + SUB_ROWS, :])
        z = (bg_ref[rows, :].astype(F32) * conv).astype(BF16)
        y_attn = _dot(attn_ref[rows, :], wba_ref[...])
        y_conv = _dot(z, wbb_ref[...])
        return (sa_ref[rows, :].astype(F32) * y_attn + sb_ref[rows, :].astype(F32) * y_conv).astype(BF16)

    def project(r, mix):
        rows = slice(r, r + SUB_ROWS)
        x1_ref[rows, :] = x_ref[rows, :] + _rmsnorm(_dot(mix, wo_ref[...]), g_mix)

    starts = list(range(0, tm, SUB_ROWS))
    mix = branches(starts[0])
    for k, r in enumerate(starts):
        nxt = branches(starts[k + 1]) if k + 1 < len(starts) else None
        project(r, mix)
        mix = nxt


def _merge(attn, p, x2, mod, g_post, w_conv, w_ba, w_bb, w_o):
    m = x2.shape[0]
    tm = TM_MERGE
    tpb = SEQ // tm
    hb = tm // HALO
    n_halo = m // HALO
    col = lambda off, w: pl.BlockSpec((tm, w), lambda i: (i, off // w))
    prev = pl.BlockSpec((HALO, CONV_W), lambda i: (jnp.maximum(i * hb - 1, 0), P_M // CONV_W))
    nxt = pl.BlockSpec((HALO, CONV_W), lambda i: (jnp.minimum((i + 1) * hb, n_halo - 1), P_M // CONV_W))
    return pl.pallas_call(
        _merge_kernel,
        grid=(m // tm,),
        in_specs=[
            pl.BlockSpec((tm, ATT_W), lambda i: (i, 0)),
            col(P_BG, CONV_W), col(P_M, CONV_W), prev, nxt,
            col(P_SA, D_MODEL), col(P_SB, D_MODEL),
            pl.BlockSpec((tm, D_MODEL), lambda i: (i, 0)),
            pl.BlockSpec((1, N_MOD, D_MODEL), lambda i: (i // tpb, 0, 0)),
            _const_spec((1, D_MODEL)),
            _const_spec((3, CONV_W)),
            _const_spec((ATT_W, D_MODEL)), _const_spec((CONV_W, D_MODEL)), _const_spec((D_MODEL, D_MODEL)),
        ],
        out_specs=pl.BlockSpec((tm, D_MODEL), lambda i: (i, 0)),
        out_shape=jax.ShapeDtypeStruct((m, D_MODEL), F32),
        scratch_shapes=[pltpu.VMEM((tm + 2 * HALO, CONV_W), F32)],
        compiler_params=_params("arbitrary"),
        name="merge",
    )(attn, p, p, p, p, p, p, x2, mod, g_post, w_conv, w_ba, w_bb, w_o)


def _ffn_kernel(x1_ref, xp_ref, xn_ref, mod_ref, gpre_ref, gpost_ref, wup_ref, wc_ref, bc_ref, wdn_ref,
                o_ref, hx_scr, g_scr, act_scr):
    tm = TM_FFN
    t = pl.program_id(0) % (SEQ // tm)
    g_in = gpre_ref[...] * (1.0 + mod_ref[0, 4:5, :])

    def normed(x):
        return (_rmsnorm(x, g_in) + mod_ref[0, 3:4, :]).astype(BF16)

    zeros = jnp.zeros((HALO, D_MODEL), BF16)
    hx_scr[0:HALO, :] = jnp.where(t > 0, normed(xp_ref[...]), zeros)
    hx_scr[HALO:HALO + tm, :] = normed(x1_ref[...])
    hx_scr[HALO + tm:, :] = jnp.where(t < SEQ // tm - 1, normed(xn_ref[...]), zeros)
    for c, n in enumerate(range(0, D_FF, FF_CHUNK)):
        g = g_scr.at[c % 2]
        g[...] = _dot(hx_scr[...], wup_ref[:, D_FF + n:D_FF + n + FF_CHUNK])
        a = _dot(hx_scr[HALO:HALO + tm, :], wup_ref[:, n:n + FF_CHUNK])
        conv = (wc_ref[0:1, n:n + FF_CHUNK] * g[HALO - 1:HALO - 1 + tm, :]
                + wc_ref[1:2, n:n + FF_CHUNK] * g[HALO:HALO + tm, :]
                + wc_ref[2:3, n:n + FF_CHUNK] * g[HALO + 1:HALO + 1 + tm, :]
                + bc_ref[:, n:n + FF_CHUNK])
        act_scr[:, n:n + FF_CHUNK] = (_gelu_tanh(conv) * a).astype(BF16)
    g_out = gpost_ref[...] * mod_ref[0, 5:6, :]
    for r in range(0, tm, SUB_ROWS):
        rows = slice(r, r + SUB_ROWS)
        y = _dot(act_scr[rows, :], wdn_ref[...])
        o_ref[rows, :] = x1_ref[rows, :] + _rmsnorm(y, g_out)


def _ffn(x1, mod, g_pre, g_post, w_up, w_conv, b_conv, w_down):
    m = x1.shape[0]
    tm = TM_FFN
    tpb = SEQ // tm
    hb = tm // HALO
    n_halo = m // HALO
    return pl.pallas_call(
        _ffn_kernel,
        grid=(m // tm,),
        in_specs=[
            pl.BlockSpec((tm, D_MODEL), lambda i: (i, 0)),
            pl.BlockSpec((HALO, D_MODEL), lambda i: (jnp.maximum(i * hb - 1, 0), 0)),
            pl.BlockSpec((HALO, D_MODEL), lambda i: (jnp.minimum((i + 1) * hb, n_halo - 1), 0)),
            pl.BlockSpec((1, N_MOD, D_MODEL), lambda i: (i // tpb, 0, 0)),
            _const_spec((1, D_MODEL)),
            _const_spec((1, D_MODEL)),
            _const_spec((D_MODEL, 2 * D_FF)),
            _const_spec((3, D_FF)),
            _const_spec((1, D_FF)),
            _const_spec((D_FF, D_MODEL)),
        ],
        out_specs=pl.BlockSpec((tm, D_MODEL), lambda i: (i, 0)),
        out_shape=jax.ShapeDtypeStruct((m, D_MODEL), F32),
        scratch_shapes=[
            pltpu.VMEM((tm + 2 * HALO, D_MODEL), BF16),
            pltpu.VMEM((2, tm + 2 * HALO, FF_CHUNK), F32),
            pltpu.VMEM((tm, D_FF), BF16),
        ],
        compiler_params=_params("arbitrary"),
        name="conv_ffn",
    )(x1, x1, x1, mod, g_pre, g_post, w_up, w_conv, b_conv, w_down)


def kernel(x, c, ctx, c_ctx, w_ada, b_ada, g_pre_mix, g_post_mix, g_pre_ffn, g_post_ffn, w_in, rpb,
           w_short_conv, w_branch_attn, w_branch_conv, w_out, w_up, w_ffn_conv, b_ffn_conv, w_down):
    depth = w_in.shape[0]
    assert depth == 1, "single-layer configuration"
    b, s, d = x.shape
    x2 = x.reshape(b * s, d)
    ctx2 = ctx.reshape(b * CTX_LEN, d)

    c_all = jnp.concatenate([c, c_ctx[None, :], jnp.zeros((ADA_ROWS - b - 1, d), F32)], axis=0)
    mod_all, bias = _ada_and_bias(c_all, w_ada[0], b_ada[0][None, :], rpb[0])
    mod = mod_all[:b].reshape(b, N_MOD, d)
    mod_c = mod_all[b].reshape(N_MOD, d)

    w_in_bf = w_in[0].astype(BF16)
    kc, vc = _ctx_kv(ctx2, g_pre_mix[0][None, :], mod_c, w_in_bf)
    p = _in_proj(x2, g_pre_mix[0][None, :], mod, w_in_bf)
    attn = _attention(p, kc, vc, bias)
    x1 = _merge(attn, p, x2, mod, g_post_mix[0][None, :], w_short_conv[0],
                w_branch_attn[0].astype(BF16), w_branch_conv[0].astype(BF16), w_out[0].astype(BF16))
    out = _ffn(x1, mod, g_pre_ffn[0][None, :], g_post_ffn[0][None, :], w_up[0].astype(BF16), w_ffn_conv[0],
               b_ffn_conv[0][None, :], w_down[0].astype(BF16))
    return out.reshape(b, s, d)
```

```python
import functools

import jax
import jax.numpy as jnp
from jax import lax
from jax.experimental import pallas as pl
from jax.experimental.pallas import tpu as pltpu

F32 = jnp.float32
BF16 = jnp.bfloat16

D_MODEL = 1024
BATCH = 16
SEQ = 2048
CTX_LEN = 256
GRID_W = 64
ROWS = SEQ // GRID_W
N_HEADS = 8
HEAD_DIM = 64
ATT_W = N_HEADS * HEAD_DIM
CONV_W = 512
WIN_H = 8
WIN_W = 16
D_FF = 2816
EPS = 1e-6
NEG_INF = -1e30

Q_OFF = 0
K_OFF = ATT_W
V_OFF = 2 * ATT_W
BG_OFF = 3 * ATT_W
CG_OFF = BG_OFF + CONV_W
HV_OFF = CG_OFF + CONV_W
GA_OFF = HV_OFF + CONV_W
GB_OFF = GA_OFF + D_MODEL
IN_W = GB_OFF + D_MODEL

P_Q = Q_OFF
P_K = K_OFF
P_V = V_OFF
P_W = BG_OFF
GATE_W = IN_W - BG_OFF
G_BG = 0
G_CG = CG_OFF - BG_OFF
G_HV = HV_OFF - BG_OFF
G_GA = GA_OFF - BG_OFF
G_GB = GB_OFF - BG_OFF

N_MOD = 6
ADA_ROWS = 24
HALO = 16
HEAD_PAIR_W = 2 * HEAD_DIM
N_PAIRS = N_HEADS // 2
WIN_KEYS = WIN_H * GRID_W
ALL_KEYS = WIN_KEYS + CTX_LEN
LOG2_E = 1.4426950408889634
QK_SCALE = HEAD_DIM ** -0.5 * LOG2_E

VMEM_LIMIT_BYTES = 56 * 1024 * 1024

ADA_TN = 512
TM_IN = 2048
TM_CTX = 512
TM_MERGE = 1024
TM_FFN = 1024
IN_CHUNK = 512
FF_CHUNK = 256
SUB_ROWS = 256
ATTN_ROWS_PER_STEP = 8
ATTN_LOOKAHEAD = 4


def _params(*sem):
    return pltpu.CompilerParams(dimension_semantics=sem, vmem_limit_bytes=VMEM_LIMIT_BYTES)


def _const_spec(shape):
    nd = len(shape)
    return pl.BlockSpec(shape, lambda *_: (0,) * nd, pipeline_mode=pl.Buffered(1))


def _dot(a, b):
    return jnp.dot(a, b, preferred_element_type=F32)


def _dot_nt(a, b):
    return lax.dot_general(a, b, (((1,), (1,)), ((), ())), preferred_element_type=F32)


def _sigmoid(x):
    return 1.0 / (1.0 + jnp.exp(-x))


def _rmsnorm(x, g):
    ms = jnp.mean(x * x, axis=-1, keepdims=True)
    return x * lax.rsqrt(ms + EPS) * g


def _gelu_tanh(x):
    c = 0.7978845608028654
    return x * (0.5 * (1.0 + jnp.tanh(c * (x + 0.044715 * (x * x * x)))))


def _write_bias(rpb_ref, h, o_ref):
    qcol = lax.broadcasted_iota(jnp.int32, (GRID_W, GRID_W), 0)
    kcol = lax.broadcasted_iota(jnp.int32, (GRID_W, GRID_W), 1)
    cs = jnp.clip(qcol - WIN_W // 2, 0, GRID_W - WIN_W)
    valid = (kcol >= cs) & (kcol < cs + WIN_W)
    dc = jnp.clip(kcol - qcol + WIN_W - 1, 0, 2 * WIN_W - 2)
    for dr in range(2 * WIN_H - 1):
        slab = jnp.zeros((GRID_W, GRID_W), F32)
        for k in range(2 * WIN_W - 1):
            slab = jnp.where(dc == k, rpb_ref[h, dr, k], slab)
        slab = jnp.where(valid, slab * LOG2_E, NEG_INF)
        for case in range(WIN_H):
            j = dr - (WIN_H - 1) + case
            if 0 <= j < WIN_H:
                o_ref[case, 0, :, j * GRID_W:(j + 1) * GRID_W] = slab


def _ada_kernel(c_ref, w_ref, b_ref, rpb_ref, o_ref, bias_ref):
    j = pl.program_id(0)
    c = c_ref[...]
    a = (c * _sigmoid(c)).astype(BF16)
    o_ref[...] = _dot(a, w_ref[...].astype(BF16)) + b_ref[...]

    @pl.when(j < N_HEADS)
    def _():
        _write_bias(rpb_ref, j, bias_ref)


def _ada_and_bias(c_all, w_ada, b_ada, rpb):
    n = w_ada.shape[1]
    assert n // ADA_TN >= N_HEADS
    return pl.pallas_call(
        _ada_kernel,
        grid=(n // ADA_TN,),
        in_specs=[
            _const_spec((ADA_ROWS, D_MODEL)),
            pl.BlockSpec((D_MODEL, ADA_TN), lambda j: (0, j)),
            pl.BlockSpec((1, ADA_TN), lambda j: (0, j)),
            pl.BlockSpec(memory_space=pltpu.SMEM),
        ],
        out_specs=[
            pl.BlockSpec((ADA_ROWS, ADA_TN), lambda j: (0, j)),
            pl.BlockSpec((WIN_H, 1, GRID_W, WIN_KEYS), lambda j: (0, jnp.minimum(j, N_HEADS - 1), 0, 0)),
        ],
        out_shape=[
            jax.ShapeDtypeStruct((ADA_ROWS, n), F32),
            jax.ShapeDtypeStruct((WIN_H, N_HEADS, GRID_W, WIN_KEYS), F32),
        ],
        compiler_params=_params("arbitrary"),
        name="ada_bias",
    )(c_all, w_ada, b_ada, rpb)


def _ctx_kernel(ctx_ref, g_ref, modc_ref, wk_ref, wv_ref, kc_ref, vc_ref):
    y = _rmsnorm(ctx_ref[...], g_ref[...])
    h = (y * (1.0 + modc_ref[1:2, :]) + modc_ref[0:1, :]).astype(BF16)
    kc_ref[...] = _dot(h, wk_ref[...]).astype(BF16)
    vc_ref[...] = _dot(h, wv_ref[...]).astype(BF16)


def _ctx_kv(ctx2, g_pre, mod_c, w_in_bf):
    m = ctx2.shape[0]
    return pl.pallas_call(
        _ctx_kernel,
        grid=(m // TM_CTX,),
        in_specs=[
            pl.BlockSpec((TM_CTX, D_MODEL), lambda i: (i, 0)),
            _const_spec((1, D_MODEL)),
            _const_spec((N_MOD, D_MODEL)),
            pl.BlockSpec((D_MODEL, ATT_W), lambda i: (0, K_OFF // ATT_W), pipeline_mode=pl.Buffered(1)),
            pl.BlockSpec((D_MODEL, ATT_W), lambda i: (0, V_OFF // ATT_W), pipeline_mode=pl.Buffered(1)),
        ],
        out_specs=[
            pl.BlockSpec((TM_CTX, ATT_W), lambda i: (i, 0)),
            pl.BlockSpec((TM_CTX, ATT_W), lambda i: (i, 0)),
        ],
        out_shape=[jax.ShapeDtypeStruct((m, ATT_W), BF16)] * 2,
        compiler_params=_params("arbitrary"),
        name="ctx_kv",
    )(ctx2, g_pre, mod_c, w_in_bf, w_in_bf)


def _in_kernel(x_ref, g_ref, mod_ref, w_ref, p_ref):
    y = _rmsnorm(x_ref[...], g_ref[...])
    h = (y * (1.0 + mod_ref[0, 1:2, :]) + mod_ref[0, 0:1, :]).astype(BF16)
    def proj(off, width=IN_CHUNK):
        return _dot(h, w_ref[:, off:off + width])

    def put(off, val):
        p_ref[:, off:off + val.shape[1]] = val.astype(BF16)

    for n in range(0, ATT_W, IN_CHUNK):
        put(P_Q + n, proj(Q_OFF + n) * QK_SCALE)
        put(P_K + n, proj(K_OFF + n))
        put(P_V + n, proj(V_OFF + n))


def _in_proj(x2, g_pre, mod, w_qkv):
    m = x2.shape[0]
    tpb = SEQ // TM_IN
    return pl.pallas_call(
        _in_kernel,
        grid=(m // TM_IN,),
        in_specs=[
            pl.BlockSpec((TM_IN, D_MODEL), lambda i: (i, 0)),
            _const_spec((1, D_MODEL)),
            pl.BlockSpec((1, N_MOD, D_MODEL), lambda i: (i // tpb, 0, 0)),
            _const_spec((D_MODEL, P_W)),
        ],
        out_specs=pl.BlockSpec((TM_IN, P_W), lambda i: (i, 0)),
        out_shape=jax.ShapeDtypeStruct((m, P_W), BF16),
        compiler_params=_params("arbitrary"),
        name="in_proj",
    )(x2, g_pre, mod, w_qkv)


def _attn_kernel(q_ref, k_ref, v_ref, kc_ref, vc_ref, bias_ref, o_ref):
    lane = lax.broadcasted_iota(jnp.int32, (GRID_W, HEAD_PAIR_W), 1)
    first_head = lane < HEAD_DIM
    zero = jnp.zeros((GRID_W, HEAD_PAIR_W), BF16)

    def offsets(r):
        rs = jnp.clip(r - WIN_H // 2, 0, ROWS - WIN_H)
        return r - rs, pl.multiple_of(r * GRID_W, GRID_W), pl.multiple_of(rs * GRID_W, GRID_W)

    def scores(r, pr):
        case, q0, k0 = offsets(r)
        cols = slice(pr * HEAD_PAIR_W, (pr + 1) * HEAD_PAIR_W)
        qp = q_ref[pl.ds(q0, GRID_W), cols]
        qm = jnp.concatenate([jnp.where(first_head, qp, zero), jnp.where(first_head, zero, qp)], axis=0)
        keys = jnp.concatenate([k_ref[pl.ds(k0, WIN_KEYS), cols], kc_ref[:, cols]], axis=0)
        s = _dot_nt(qm, keys)
        bias = jnp.concatenate([bias_ref[case, 2 * pr], bias_ref[case, 2 * pr + 1]], axis=0)
        return jnp.concatenate([s[:, :WIN_KEYS] + bias, s[:, WIN_KEYS:]], axis=1)

    def outputs(r, pr, s):
        _, q0, k0 = offsets(r)
        cols = slice(pr * HEAD_PAIR_W, (pr + 1) * HEAD_PAIR_W)
        e = jnp.exp2(s - jnp.max(s, axis=-1, keepdims=True))
        l = jnp.sum(e, axis=-1, keepdims=True)
        vals = jnp.concatenate([v_ref[pl.ds(k0, WIN_KEYS), cols], vc_ref[:, cols]], axis=0)
        o2 = _dot(e.astype(BF16), vals) / l
        o = jnp.where(first_head, o2[:GRID_W], o2[GRID_W:])
        o_ref[pl.ds(q0, GRID_W), cols] = o.astype(BF16)

    def rows(i, carry):
        items = [(i * ATTN_ROWS_PER_STEP + dr, pr) for dr in range(ATTN_ROWS_PER_STEP) for pr in range(N_PAIRS)]
        pending = [scores(*it) for it in items[:ATTN_LOOKAHEAD]]
        for n, it in enumerate(items):
            if n + ATTN_LOOKAHEAD < len(items):
                pending.append(scores(*items[n + ATTN_LOOKAHEAD]))
            outputs(*it, pending.pop(0))
        return carry

    lax.fori_loop(0, ROWS // ATTN_ROWS_PER_STEP, rows, 0)


def _attention(p, kc, vc, bias):
    seq_spec = lambda col: pl.BlockSpec((SEQ, ATT_W), lambda b: (b, col))
    ctx_spec = pl.BlockSpec((CTX_LEN, ATT_W), lambda b: (b, 0))
    return pl.pallas_call(
        _attn_kernel,
        grid=(BATCH,),
        in_specs=[
            seq_spec(P_Q // ATT_W), seq_spec(P_K // ATT_W), seq_spec(P_V // ATT_W),
            ctx_spec, ctx_spec,
            _const_spec((WIN_H, N_HEADS, GRID_W, WIN_KEYS)),
        ],
        out_specs=pl.BlockSpec((SEQ, ATT_W), lambda b: (b, 0)),
        out_shape=jax.ShapeDtypeStruct((BATCH * SEQ, ATT_W), BF16),
        compiler_params=_params("arbitrary"),
        name="attention",
    )(p, p, p, kc, vc, bias)


def _merge_kernel(attn_ref, x_ref, xp_ref, xn_ref, mod_ref, gpre_ref, gpost_ref, wg_ref, wc_ref,
                  wba_ref, wbb_ref, wo_ref, x1_ref, hx_scr, m_scr):
    tm = TM_MERGE
    t = pl.program_id(0) % (SEQ // tm)
    g_in = gpre_ref[...] * (1.0 + mod_ref[0, 1:2, :])
    g_mix = gpost_ref[...] * mod_ref[0, 2:3, :]

    def normed(x):
        return (_rmsnorm(x, g_in) + mod_ref[0, 0:1, :]).astype(BF16)

    zeros = jnp.zeros((HALO, D_MODEL), BF16)
    hx_scr[0:HALO, :] = jnp.where(t > 0, normed(xp_ref[...]), zeros)
    hx_scr[HALO:HALO + tm, :] = normed(x_ref[...])
    hx_scr[HALO + tm:, :] = jnp.where(t < SEQ // tm - 1, normed(xn_ref[...]), zeros)
    m_scr[...] = (_dot(hx_scr[...], wg_ref[:, G_CG:G_CG + CONV_W])
                  * _dot(hx_scr[...], wg_ref[:, G_HV:G_HV + CONV_W]))

    def branches(r):
        rows = slice(r, r + SUB_ROWS)
        h = hx_scr[HALO + r:HALO + r + SUB_ROWS, :]
        conv = (wc_ref[0:1, :] * m_scr[HALO - 1 + r:HALO - 1 + r + SUB_ROWS, :]
                + wc_ref[1:2, :] * m_scr[HALO + r:HALO + r + SUB_ROWS, :]
                + wc_ref[2:3, :] * m_scr[HALO + 1 + r:HALO + 1 + r + SUB_ROWS, :])
        z = (_dot(h, wg_ref[:, G_BG:G_BG + CONV_W]) * conv).astype(BF16)
        gate_attn = _sigmoid(_dot(h, wg_ref[:, G_GA:G_GA + D_MODEL]))
        gate_conv = _sigmoid(_dot(h, wg_ref[:, G_GB:G_GB + D_MODEL]))
        y_attn = _dot(attn_ref[rows, :], wba_ref[...])
        y_conv = _dot(z, wbb_ref[...])
        return (gate_attn * y_attn + gate_conv * y_conv).astype(BF16)

    def project(r, mix):
        rows = slice(r, r + SUB_ROWS)
        x1_ref[rows, :] = x_ref[rows, :] + _rmsnorm(_dot(mix, wo_ref[...]), g_mix)

    starts = list(range(0, tm, SUB_ROWS))
    mix = branches(starts[0])
    for k, r in enumerate(starts):
        nxt = branches(starts[k + 1]) if k + 1 < len(starts) else None
        project(r, mix)
        mix = nxt


def _merge(attn, x2, mod, g_pre, g_post, w_gate, w_conv, w_ba, w_bb, w_o):
    m = x2.shape[0]
    tm = TM_MERGE
    tpb = SEQ // tm
    hb = tm // HALO
    n_halo = m // HALO
    return pl.pallas_call(
        _merge_kernel,
        grid=(m // tm,),
        in_specs=[
            pl.BlockSpec((tm, ATT_W), lambda i: (i, 0)),
            pl.BlockSpec((tm, D_MODEL), lambda i: (i, 0)),
            pl.BlockSpec((HALO, D_MODEL), lambda i: (jnp.maximum(i * hb - 1, 0), 0)),
            pl.BlockSpec((HALO, D_MODEL), lambda i: (jnp.minimum((i + 1) * hb, n_halo - 1), 0)),
            pl.BlockSpec((1, N_MOD, D_MODEL), lambda i: (i // tpb, 0, 0)),
            _const_spec((1, D_MODEL)), _const_spec((1, D_MODEL)),
            _const_spec((D_MODEL, GATE_W)),
            _const_spec((3, CONV_W)),
            _const_spec((ATT_W, D_MODEL)), _const_spec((CONV_W, D_MODEL)), _const_spec((D_MODEL, D_MODEL)),
        ],
        out_specs=pl.BlockSpec((tm, D_MODEL), lambda i: (i, 0)),
        out_shape=jax.ShapeDtypeStruct((m, D_MODEL), F32),
        scratch_shapes=[
            pltpu.VMEM((tm + 2 * HALO, D_MODEL), BF16),
            pltpu.VMEM((tm + 2 * HALO, CONV_W), F32),
        ],
        compiler_params=_params("arbitrary"),
        name="merge",
    )(attn, x2, x2, x2, mod, g_pre, g_post, w_gate, w_conv, w_ba, w_bb, w_o)


def _ffn_kernel(x1_ref, xp_ref, xn_ref, mod_ref, gpre_ref, gpost_ref, wup_ref, wc_ref, bc_ref, wdn_ref,
                o_ref, hx_scr, g_scr, act_scr):
    tm = TM_FFN
    t = pl.program_id(0) % (SEQ // tm)
    g_in = gpre_ref[...] * (1.0 + mod_ref[0, 4:5, :])

    def normed(x):
        return (_rmsnorm(x, g_in) + mod_ref[0, 3:4, :]).astype(BF16)

    zeros = jnp.zeros((HALO, D_MODEL), BF16)
    hx_scr[0:HALO, :] = jnp.where(t > 0, normed(xp_ref[...]), zeros)
    hx_scr[HALO:HALO + tm, :] = normed(x1_ref[...])
    hx_scr[HALO + tm:, :] = jnp.where(t < SEQ // tm - 1, normed(xn_ref[...]), zeros)
    for c, n in enumerate(range(0, D_FF, FF_CHUNK)):
        g = g_scr.at[c % 2]
        g[...] = _dot(hx_scr[...], wup_ref[:, D_FF + n:D_FF + n + FF_CHUNK])
        a = _dot(hx_scr[HALO:HALO + tm, :], wup_ref[:, n:n + FF_CHUNK])
        conv = (wc_ref[0:1, n:n + FF_CHUNK] * g[HALO - 1:HALO - 1 + tm, :]
                + wc_ref[1:2, n:n + FF_CHUNK] * g[HALO:HALO + tm, :]
                + wc_ref[2:3, n:n + FF_CHUNK] * g[HALO + 1:HALO + 1 + tm, :]
                + bc_ref[:, n:n + FF_CHUNK])
        act_scr[:, n:n + FF_CHUNK] = (_gelu_tanh(conv) * a).astype(BF16)
    g_out = gpost_ref[...] * mod_ref[0, 5:6, :]
    for r in range(0, tm, SUB_ROWS):
        rows = slice(r, r + SUB_ROWS)
        y = _dot(act_scr[rows, :], wdn_ref[...])
        o_ref[rows, :] = x1_ref[rows, :] + _rmsnorm(y, g_out)


def _ffn(x1, mod, g_pre, g_post, w_up, w_conv, b_conv, w_down):
    m = x1.shape[0]
    tm = TM_FFN
    tpb = SEQ // tm
    hb = tm // HALO
    n_halo = m // HALO
    return pl.pallas_call(
        _ffn_kernel,
        grid=(m // tm,),
        in_specs=[
            pl.BlockSpec((tm, D_MODEL), lambda i: (i, 0)),
            pl.BlockSpec((HALO, D_MODEL), lambda i: (jnp.maximum(i * hb - 1, 0), 0)),
            pl.BlockSpec((HALO, D_MODEL), lambda i: (jnp.minimum((i + 1) * hb, n_halo - 1), 0)),
            pl.BlockSpec((1, N_MOD, D_MODEL), lambda i: (i // tpb, 0, 0)),
            _const_spec((1, D_MODEL)),
            _const_spec((1, D_MODEL)),
            _const_spec((D_MODEL, 2 * D_FF)),
            _const_spec((3, D_FF)),
            _const_spec((1, D_FF)),
            _const_spec((D_FF, D_MODEL)),
        ],
        out_specs=pl.BlockSpec((tm, D_MODEL), lambda i: (i, 0)),
        out_shape=jax.ShapeDtypeStruct((m, D_MODEL), F32),
        scratch_shapes=[
            pltpu.VMEM((tm + 2 * HALO, D_MODEL), BF16),
            pltpu.VMEM((2, tm + 2 * HALO, FF_CHUNK), F32),
            pltpu.VMEM((tm, D_FF), BF16),
        ],
        compiler_params=_params("arbitrary"),
        name="conv_ffn",
    )(x1, x1, x1, mod, g_pre, g_post, w_up, w_conv, b_conv, w_down)


def kernel(x, c, ctx, c_ctx, w_ada, b_ada, g_pre_mix, g_post_mix, g_pre_ffn, g_post_ffn, w_in, rpb,
           w_short_conv, w_branch_attn, w_branch_conv, w_out, w_up, w_ffn_conv, b_ffn_conv, w_down):
    depth = w_in.shape[0]
    assert depth == 1, "single-layer configuration"
    b, s, d = x.shape
    x2 = x.reshape(b * s, d)
    ctx2 = ctx.reshape(b * CTX_LEN, d)

    c_all = jnp.concatenate([c, c_ctx[None, :], jnp.zeros((ADA_ROWS - b - 1, d), F32)], axis=0)
    mod_all, bias = _ada_and_bias(c_all, w_ada[0], b_ada[0][None, :], rpb[0])
    mod = mod_all[:b].reshape(b, N_MOD, d)
    mod_c = mod_all[b].reshape(N_MOD, d)

    w_qkv = w_in[0][:, :BG_OFF].astype(BF16)
    w_gate = w_in[0][:, BG_OFF:].astype(BF16)
    kc, vc = _ctx_kv(ctx2, g_pre_mix[0][None, :], mod_c, w_qkv)
    p = _in_proj(x2, g_pre_mix[0][None, :], mod, w_qkv)
    attn = _attention(p, kc, vc, bias)
    x1 = _merge(attn, x2, mod, g_pre_mix[0][None, :], g_post_mix[0][None, :], w_gate, w_short_conv[0],
                w_branch_attn[0].astype(BF16), w_branch_conv[0].astype(BF16), w_out[0].astype(BF16))
    out = _ffn(x1, mod, g_pre_ffn[0][None, :], g_post_ffn[0][None, :], w_up[0].astype(BF16), w_ffn_conv[0],
               b_ffn_conv[0][None, :], w_down[0].astype(BF16))
    return out.reshape(b, s, d)
```

```python
import functools

import jax
import jax.numpy as jnp
from jax import lax
from jax.experimental import pallas as pl
from jax.experimental.pallas import tpu as pltpu

F32 = jnp.float32
BF16 = jnp.bfloat16

D_MODEL = 1024
BATCH = 16
SEQ = 2048
CTX_LEN = 256
GRID_W = 64
ROWS = SEQ // GRID_W
N_HEADS = 8
HEAD_DIM = 64
ATT_W = N_HEADS * HEAD_DIM
CONV_W = 512
WIN_H = 8
WIN_W = 16
D_FF = 2816
EPS = 1e-6
NEG_INF = -1e30

Q_OFF = 0
K_OFF = ATT_W
V_OFF = 2 * ATT_W
BG_OFF = 3 * ATT_W
CG_OFF = BG_OFF + CONV_W
HV_OFF = CG_OFF + CONV_W
GA_OFF = HV_OFF + CONV_W
GB_OFF = GA_OFF + D_MODEL
IN_W = GB_OFF + D_MODEL

GATE_W = IN_W - BG_OFF
G_BG = 0
G_CG = CG_OFF - BG_OFF
G_HV = HV_OFF - BG_OFF
G_GA = GA_OFF - BG_OFF
G_GB = GB_OFF - BG_OFF

N_MOD = 6
ADA_ROWS = 24
HALO = 16
HEAD_PAIR_W = 2 * HEAD_DIM
N_PAIRS = N_HEADS // 2
WIN_KEYS = WIN_H * GRID_W
ALL_KEYS = WIN_KEYS + CTX_LEN
LOG2_E = 1.4426950408889634
QK_SCALE = HEAD_DIM ** -0.5 * LOG2_E

VMEM_LIMIT_BYTES = 56 * 1024 * 1024

ADA_TN = 512
QKV_ROWS = 512
TM_MERGE = 1024
TM_FFN = 1024
FF_CHUNK = 256
FF_CHUNKS = (FF_CHUNK,) * (D_FF // FF_CHUNK) + ((D_FF % FF_CHUNK,) if D_FF % FF_CHUNK else ())
SUB_ROWS = 256
FFN_SUB_ROWS = 256
ATTN_ROWS_PER_STEP = 8
ATTN_LOOKAHEAD = 4


def _params(*sem):
    return pltpu.CompilerParams(dimension_semantics=sem, vmem_limit_bytes=VMEM_LIMIT_BYTES)


def _const_spec(shape):
    nd = len(shape)
    return pl.BlockSpec(shape, lambda *_: (0,) * nd, pipeline_mode=pl.Buffered(1))


def _dot(a, b):
    return jnp.dot(a, b, preferred_element_type=F32)


def _dot_nt(a, b):
    return lax.dot_general(a, b, (((1,), (1,)), ((), ())), preferred_element_type=F32)


def _sigmoid(x):
    return 1.0 / (1.0 + jnp.exp(-x))


def _rmsnorm(x, g):
    ms = jnp.mean(x * x, axis=-1, keepdims=True)
    return x * lax.rsqrt(ms + EPS) * g


def _gelu_tanh(x):
    c = 0.7978845608028654
    return x * (0.5 * (1.0 + jnp.tanh(c * (x + 0.044715 * (x * x * x)))))


def _write_bias(rpb_ref, h, o_ref):
    qcol = lax.broadcasted_iota(jnp.int32, (GRID_W, GRID_W), 0)
    kcol = lax.broadcasted_iota(jnp.int32, (GRID_W, GRID_W), 1)
    cs = jnp.clip(qcol - WIN_W // 2, 0, GRID_W - WIN_W)
    valid = (kcol >= cs) & (kcol < cs + WIN_W)
    dc = jnp.clip(kcol - qcol + WIN_W - 1, 0, 2 * WIN_W - 2)
    for dr in range(2 * WIN_H - 1):
        slab = jnp.zeros((GRID_W, GRID_W), F32)
        for k in range(2 * WIN_W - 1):
            slab = jnp.where(dc == k, rpb_ref[h, dr, k], slab)
        slab = jnp.where(valid, slab * LOG2_E, NEG_INF)
        for case in range(WIN_H):
            j = dr - (WIN_H - 1) + case
            if 0 <= j < WIN_H:
                o_ref[case, 0, :, j * GRID_W:(j + 1) * GRID_W] = slab


def _ada_kernel(c_ref, w_ref, b_ref, rpb_ref, o_ref, bias_ref):
    j = pl.program_id(0)
    c = c_ref[...]
    a = (c * _sigmoid(c)).astype(BF16)
    o_ref[...] = _dot(a, w_ref[...].astype(BF16)) + b_ref[...]

    @pl.when(j < N_HEADS)
    def _():
        _write_bias(rpb_ref, j, bias_ref)


def _ada_and_bias(c_all, w_ada, b_ada, rpb):
    n = w_ada.shape[1]
    assert n // ADA_TN >= N_HEADS
    return pl.pallas_call(
        _ada_kernel,
        grid=(n // ADA_TN,),
        in_specs=[
            _const_spec((ADA_ROWS, D_MODEL)),
            pl.BlockSpec((D_MODEL, ADA_TN), lambda j: (0, j)),
            pl.BlockSpec((1, ADA_TN), lambda j: (0, j)),
            pl.BlockSpec(memory_space=pltpu.SMEM),
        ],
        out_specs=[
            pl.BlockSpec((ADA_ROWS, ADA_TN), lambda j: (0, j)),
            pl.BlockSpec((WIN_H, 1, GRID_W, WIN_KEYS), lambda j: (0, jnp.minimum(j, N_HEADS - 1), 0, 0)),
        ],
        out_shape=[
            jax.ShapeDtypeStruct((ADA_ROWS, n), F32),
            jax.ShapeDtypeStruct((WIN_H, N_HEADS, GRID_W, WIN_KEYS), F32),
        ],
        compiler_params=_params("arbitrary"),
        name="ada_bias",
    )(c_all, w_ada, b_ada, rpb)


def _attn_kernel(x_ref, ctx_ref, mod_ref, modc_ref, gpre_ref, w_ref, bias_ref, o_ref,
                 q_ref, k_ref, v_ref, kc_ref, vc_ref):
    def project(src_ref, dst_refs, shift, scale, rows, first_col):
        g_in = gpre_ref[...] * (1.0 + scale)
        step = min(QKV_ROWS, rows)
        for r in range(0, rows, step):
            h = (_rmsnorm(src_ref[r:r + step, :], g_in) + shift).astype(BF16)
            for j, dst in enumerate(dst_refs):
                col = first_col + j * ATT_W
                val = _dot(h, w_ref[:, col:col + ATT_W])
                if col == Q_OFF:
                    val = val * QK_SCALE
                dst[r:r + step, :] = val.astype(BF16)

    project(ctx_ref, (kc_ref, vc_ref), modc_ref[0:1, :], modc_ref[1:2, :], CTX_LEN, K_OFF)
    project(x_ref, (q_ref, k_ref, v_ref), mod_ref[0, 0:1, :], mod_ref[0, 1:2, :], SEQ, Q_OFF)

    lane = lax.broadcasted_iota(jnp.int32, (GRID_W, HEAD_PAIR_W), 1)
    first_head = lane < HEAD_DIM
    zero = jnp.zeros((GRID_W, HEAD_PAIR_W), BF16)

    def offsets(r):
        rs = jnp.clip(r - WIN_H // 2, 0, ROWS - WIN_H)
        return r - rs, pl.multiple_of(r * GRID_W, GRID_W), pl.multiple_of(rs * GRID_W, GRID_W)

    def scores(r, pr):
        case, q0, k0 = offsets(r)
        cols = slice(pr * HEAD_PAIR_W, (pr + 1) * HEAD_PAIR_W)
        qp = q_ref[pl.ds(q0, GRID_W), cols]
        qm = jnp.concatenate([jnp.where(first_head, qp, zero), jnp.where(first_head, zero, qp)], axis=0)
        keys = jnp.concatenate([k_ref[pl.ds(k0, WIN_KEYS), cols], kc_ref[:, cols]], axis=0)
        s = _dot_nt(qm, keys)
        bias = jnp.concatenate([bias_ref[case, 2 * pr], bias_ref[case, 2 * pr + 1]], axis=0)
        return jnp.concatenate([s[:, :WIN_KEYS] + bias, s[:, WIN_KEYS:]], axis=1)

    def outputs(r, pr, s):
        _, q0, k0 = offsets(r)
        cols = slice(pr * HEAD_PAIR_W, (pr + 1) * HEAD_PAIR_W)
        e = jnp.exp2(s - jnp.max(s, axis=-1, keepdims=True))
        l = jnp.sum(e, axis=-1, keepdims=True)
        vals = jnp.concatenate([v_ref[pl.ds(k0, WIN_KEYS), cols], vc_ref[:, cols]], axis=0)
        o2 = _dot(e.astype(BF16), vals) / l
        o = jnp.where(first_head, o2[:GRID_W], o2[GRID_W:])
        o_ref[pl.ds(q0, GRID_W), cols] = o.astype(BF16)

    def rows(i, carry):
        items = [(i * ATTN_ROWS_PER_STEP + dr, pr) for dr in range(ATTN_ROWS_PER_STEP) for pr in range(N_PAIRS)]
        pending = [scores(*it) for it in items[:ATTN_LOOKAHEAD]]
        for n, it in enumerate(items):
            if n + ATTN_LOOKAHEAD < len(items):
                pending.append(scores(*items[n + ATTN_LOOKAHEAD]))
            outputs(*it, pending.pop(0))
        return carry

    lax.fori_loop(0, ROWS // ATTN_ROWS_PER_STEP, rows, 0)


def _attention(x2, ctx2, mod, mod_c, g_pre, w_qkv, bias):
    return pl.pallas_call(
        _attn_kernel,
        grid=(BATCH,),
        in_specs=[
            pl.BlockSpec((SEQ, D_MODEL), lambda b: (b, 0)),
            pl.BlockSpec((CTX_LEN, D_MODEL), lambda b: (b, 0)),
            pl.BlockSpec((1, N_MOD, D_MODEL), lambda b: (b, 0, 0)),
            _const_spec((N_MOD, D_MODEL)),
            _const_spec((1, D_MODEL)),
            _const_spec((D_MODEL, 3 * ATT_W)),
            _const_spec((WIN_H, N_HEADS, GRID_W, WIN_KEYS)),
        ],
        out_specs=pl.BlockSpec((SEQ, ATT_W), lambda b: (b, 0)),
        out_shape=jax.ShapeDtypeStruct((BATCH * SEQ, ATT_W), BF16),
        scratch_shapes=[pltpu.VMEM((SEQ, ATT_W), BF16)] * 3 + [pltpu.VMEM((CTX_LEN, ATT_W), BF16)] * 2,
        compiler_params=_params("arbitrary"),
        name="attention",
    )(x2, ctx2, mod, mod_c, g_pre, w_qkv, bias)


def _merge_kernel(attn_ref, x_ref, xp_ref, xn_ref, mod_ref, gpre_ref, gpost_ref, wg_ref, wc_ref,
                  wba_ref, wbb_ref, wo_ref, x1_ref, hx_scr, m_scr):
    tm = TM_MERGE
    t = pl.program_id(0) % (SEQ // tm)
    g_in = gpre_ref[...] * (1.0 + mod_ref[0, 1:2, :])
    g_mix = gpost_ref[...] * mod_ref[0, 2:3, :]

    def normed(x):
        return (_rmsnorm(x, g_in) + mod_ref[0, 0:1, :]).astype(BF16)

    zeros = jnp.zeros((HALO, D_MODEL), BF16)
    hx_scr[0:HALO, :] = jnp.where(t > 0, normed(xp_ref[...]), zeros)
    hx_scr[HALO:HALO + tm, :] = normed(x_ref[...])
    hx_scr[HALO + tm:, :] = jnp.where(t < SEQ // tm - 1, normed(xn_ref[...]), zeros)
    m_scr[...] = (_dot(hx_scr[...], wg_ref[:, G_CG:G_CG + CONV_W])
                  * _dot(hx_scr[...], wg_ref[:, G_HV:G_HV + CONV_W]))

    def branches(r):
        rows = slice(r, r + SUB_ROWS)
        h = hx_scr[HALO + r:HALO + r + SUB_ROWS, :]
        conv = (wc_ref[0:1, :] * m_scr[HALO - 1 + r:HALO - 1 + r + SUB_ROWS, :]
                + wc_ref[1:2, :] * m_scr[HALO + r:HALO + r + SUB_ROWS, :]
                + wc_ref[2:3, :] * m_scr[HALO + 1 + r:HALO + 1 + r + SUB_ROWS, :])
        z = (_dot(h, wg_ref[:, G_BG:G_BG + CONV_W]) * conv).astype(BF16)
        gate_attn = _sigmoid(_dot(h, wg_ref[:, G_GA:G_GA + D_MODEL]))
        gate_conv = _sigmoid(_dot(h, wg_ref[:, G_GB:G_GB + D_MODEL]))
        y_attn = _dot(attn_ref[rows, :], wba_ref[...])
        y_conv = _dot(z, wbb_ref[...])
        return (gate_attn * y_attn + gate_conv * y_conv).astype(BF16)

    def project(r, mix):
        rows = slice(r, r + SUB_ROWS)
        x1_ref[rows, :] = x_ref[rows, :] + _rmsnorm(_dot(mix, wo_ref[...]), g_mix)

    starts = list(range(0, tm, SUB_ROWS))
    mix = branches(starts[0])
    for k, r in enumerate(starts):
        nxt = branches(starts[k + 1]) if k + 1 < len(starts) else None
        project(r, mix)
        mix = nxt


def _merge(attn, x2, mod, g_pre, g_post, w_gate, w_conv, w_ba, w_bb, w_o):
    m = x2.shape[0]
    tm = TM_MERGE
    tpb = SEQ // tm
    hb = tm // HALO
    n_halo = m // HALO
    return pl.pallas_call(
        _merge_kernel,
        grid=(m // tm,),
        in_specs=[
            pl.BlockSpec((tm, ATT_W), lambda i: (i, 0)),
            pl.BlockSpec((tm, D_MODEL), lambda i: (i, 0)),
            pl.BlockSpec((HALO, D_MODEL), lambda i: (jnp.maximum(i * hb - 1, 0), 0)),
            pl.BlockSpec((HALO, D_MODEL), lambda i: (jnp.minimum((i + 1) * hb, n_halo - 1), 0)),
            pl.BlockSpec((1, N_MOD, D_MODEL), lambda i: (i // tpb, 0, 0)),
            _const_spec((1, D_MODEL)), _const_spec((1, D_MODEL)),
            _const_spec((D_MODEL, GATE_W)),
            _const_spec((3, CONV_W)),
            _const_spec((ATT_W, D_MODEL)), _const_spec((CONV_W, D_MODEL)), _const_spec((D_MODEL, D_MODEL)),
        ],
        out_specs=pl.BlockSpec((tm, D_MODEL), lambda i: (i, 0)),
        out_shape=jax.ShapeDtypeStruct((m, D_MODEL), F32),
        scratch_shapes=[
            pltpu.VMEM((tm + 2 * HALO, D_MODEL), BF16),
            pltpu.VMEM((tm + 2 * HALO, CONV_W), F32),
        ],
        compiler_params=_params("arbitrary"),
        name="merge",
    )(attn, x2, x2, x2, mod, g_pre, g_post, w_gate, w_conv, w_ba, w_bb, w_o)


def _ffn_kernel(x1_ref, xp_ref, xn_ref, mod_ref, gpre_ref, gpost_ref, wup_ref, wc_ref, bc_ref, wdn_ref,
                o_ref, hx_scr, g_scr, act_scr):
    tm = TM_FFN
    t = pl.program_id(0) % (SEQ // tm)
    g_in = gpre_ref[...] * (1.0 + mod_ref[0, 4:5, :])

    def normed(x):
        return (_rmsnorm(x, g_in) + mod_ref[0, 3:4, :]).astype(BF16)

    zeros = jnp.zeros((HALO, D_MODEL), BF16)
    hx_scr[0:HALO, :] = jnp.where(t > 0, normed(xp_ref[...]), zeros)
    hx_scr[HALO:HALO + tm, :] = normed(x1_ref[...])
    hx_scr[HALO + tm:, :] = jnp.where(t < SEQ // tm - 1, normed(xn_ref[...]), zeros)
    n = 0
    for c, w in enumerate(FF_CHUNKS):
        g = g_scr.at[c % 2, :, 0:w]
        g[...] = _dot(hx_scr[...], wup_ref[:, D_FF + n:D_FF + n + w])
        a = _dot(hx_scr[HALO:HALO + tm, :], wup_ref[:, n:n + w])
        conv = (wc_ref[0:1, n:n + w] * g[HALO - 1:HALO - 1 + tm, :]
                + wc_ref[1:2, n:n + w] * g[HALO:HALO + tm, :]
                + wc_ref[2:3, n:n + w] * g[HALO + 1:HALO + 1 + tm, :]
                + bc_ref[:, n:n + w])
        act_scr[:, n:n + w] = (_gelu_tanh(conv) * a).astype(BF16)
        n += w
    g_out = gpost_ref[...] * mod_ref[0, 5:6, :]
    for r in range(0, tm, FFN_SUB_ROWS):
        rows = slice(r, r + FFN_SUB_ROWS)
        y = _dot(act_scr[rows, :], wdn_ref[...])
        o_ref[rows, :] = x1_ref[rows, :] + _rmsnorm(y, g_out)


def _ffn(x1, mod, g_pre, g_post, w_up, w_conv, b_conv, w_down):
    m = x1.shape[0]
    tm = TM_FFN
    tpb = SEQ // tm
    hb = tm // HALO
    n_halo = m // HALO
    return pl.pallas_call(
        _ffn_kernel,
        grid=(m // tm,),
        in_specs=[
            pl.BlockSpec((tm, D_MODEL), lambda i: (i, 0)),
            pl.BlockSpec((HALO, D_MODEL), lambda i: (jnp.maximum(i * hb - 1, 0), 0)),
            pl.BlockSpec((HALO, D_MODEL), lambda i: (jnp.minimum((i + 1) * hb, n_halo - 1), 0)),
            pl.BlockSpec((1, N_MOD, D_MODEL), lambda i: (i // tpb, 0, 0)),
            _const_spec((1, D_MODEL)),
            _const_spec((1, D_MODEL)),
            _const_spec((D_MODEL, 2 * D_FF)),
            _const_spec((3, D_FF)),
            _const_spec((1, D_FF)),
            _const_spec((D_FF, D_MODEL)),
        ],
        out_specs=pl.BlockSpec((tm, D_MODEL), lambda i: (i, 0)),
        out_shape=jax.ShapeDtypeStruct((m, D_MODEL), F32),
        scratch_shapes=[
            pltpu.VMEM((tm + 2 * HALO, D_MODEL), BF16),
            pltpu.VMEM((2, tm + 2 * HALO, FF_CHUNK), F32),
            pltpu.VMEM((tm, D_FF), BF16),
        ],
        compiler_params=_params("arbitrary"),
        name="conv_ffn",
    )(x1, x1, x1, mod, g_pre, g_post, w_up, w_conv, b_conv, w_down)


def kernel(x, c, ctx, c_ctx, w_ada, b_ada, g_pre_mix, g_post_mix, g_pre_ffn, g_post_ffn, w_in, rpb,
           w_short_conv, w_branch_attn, w_branch_conv, w_out, w_up, w_ffn_conv, b_ffn_conv, w_down):
    depth = w_in.shape[0]
    assert depth == 1, "single-layer configuration"
    b, s, d = x.shape
    x2 = x.reshape(b * s, d)
    ctx2 = ctx.reshape(b * CTX_LEN, d)

    c_all = jnp.concatenate([c, c_ctx[None, :], jnp.zeros((ADA_ROWS - b - 1, d), F32)], axis=0)
    mod_all, bias = _ada_and_bias(c_all, w_ada[0], b_ada[0][None, :], rpb[0])
    mod = mod_all[:b].reshape(b, N_MOD, d)
    mod_c = mod_all[b].reshape(N_MOD, d)

    w_qkv = w_in[0][:, :BG_OFF].astype(BF16)
    w_gate = w_in[0][:, BG_OFF:].astype(BF16)
    attn = _attention(x2, ctx2, mod, mod_c, g_pre_mix[0][None, :], w_qkv, bias)
    x1 = _merge(attn, x2, mod, g_pre_mix[0][None, :], g_post_mix[0][None, :], w_gate, w_short_conv[0],
                w_branch_attn[0].astype(BF16), w_branch_conv[0].astype(BF16), w_out[0].astype(BF16))
    out = _ffn(x1, mod, g_pre_ffn[0][None, :], g_post_ffn[0][None, :], w_up[0].astype(BF16), w_ffn_conv[0],
               b_ffn_conv[0][None, :], w_down[0].astype(BF16))
    return out.reshape(b, s, d)
```

```python
import jax
import jax.numpy as jnp
from jax import lax
from jax.experimental import pallas as pl
from jax.experimental.pallas import tpu as pltpu

F32 = jnp.float32
BF16 = jnp.bfloat16

D_MODEL = 1024
BATCH = 16
SEQ = 2048
CTX_LEN = 256
GRID_W = 64
ROWS = SEQ // GRID_W
N_HEADS = 8
HEAD_DIM = 64
ATT_W = N_HEADS * HEAD_DIM
CONV_W = 512
WIN_H = 8
WIN_W = 16
D_FF = 2816
EPS = 1e-6
NEG_INF = -1e30

Q_OFF = 0
K_OFF = ATT_W
V_OFF = 2 * ATT_W
BG_OFF = 3 * ATT_W
CG_OFF = BG_OFF + CONV_W
HV_OFF = CG_OFF + CONV_W
GA_OFF = HV_OFF + CONV_W
GB_OFF = GA_OFF + D_MODEL
IN_W = GB_OFF + D_MODEL

P_W = BG_OFF
GATE_W = IN_W - BG_OFF
G_BG = 0
G_CG = CG_OFF - BG_OFF
G_HV = HV_OFF - BG_OFF
G_GA = GA_OFF - BG_OFF
G_GB = GB_OFF - BG_OFF

N_MOD = 6
ADA_ROWS = 24
HALO = 16
HEAD_PAIR_W = 2 * HEAD_DIM
N_PAIRS = N_HEADS // 2
WIN_KEYS = WIN_H * GRID_W
ALL_KEYS = WIN_KEYS + CTX_LEN
LOG2_E = 1.4426950408889634
QK_SCALE = HEAD_DIM ** -0.5 * LOG2_E

VMEM_LIMIT_BYTES = 56 * 1024 * 1024

ADA_TN = 512
TM_IN = 2048
TM_CTX = 512
TM_MERGE = 1024
TM_FFN = 1024
IN_CHUNK = 512
FF_CHUNK = 256
SUB_ROWS = 256
ATTN_ROWS_PER_STEP = 8
ATTN_LOOKAHEAD = 4


def _params(*sem):
    return pltpu.CompilerParams(dimension_semantics=sem, vmem_limit_bytes=VMEM_LIMIT_BYTES)


def _const_spec(shape):
    nd = len(shape)
    return pl.BlockSpec(shape, lambda *_: (0,) * nd, pipeline_mode=pl.Buffered(1))


def _dot(a, b):
    return jnp.dot(a, b, preferred_element_type=F32)


def _dot_nt(a, b):
    return lax.dot_general(a, b, (((1,), (1,)), ((), ())), preferred_element_type=F32)


def _sigmoid(x):
    return 1.0 / (1.0 + jnp.exp(-x))


def _rmsnorm(x, g):
    ms = jnp.mean(x * x, axis=-1, keepdims=True)
    return x * lax.rsqrt(ms + EPS) * g


def _gelu_tanh(x):
    c = 0.7978845608028654
    return x * (0.5 * (1.0 + jnp.tanh(c * (x + 0.044715 * (x * x * x)))))


def _write_bias(rpb_ref, h, o_ref):
    qcol = lax.broadcasted_iota(jnp.int32, (GRID_W, GRID_W), 0)
    kcol = lax.broadcasted_iota(jnp.int32, (GRID_W, GRID_W), 1)
    cs = jnp.clip(qcol - WIN_W // 2, 0, GRID_W - WIN_W)
    valid = (kcol >= cs) & (kcol < cs + WIN_W)
    dc = jnp.clip(kcol - qcol + WIN_W - 1, 0, 2 * WIN_W - 2)
    for dr in range(2 * WIN_H - 1):
        slab = jnp.zeros((GRID_W, GRID_W), F32)
        for k in range(2 * WIN_W - 1):
            slab = jnp.where(dc == k, rpb_ref[h, dr, k], slab)
        slab = jnp.where(valid, slab * LOG2_E, NEG_INF)
        for case in range(WIN_H):
            j = dr - (WIN_H - 1) + case
            if 0 <= j < WIN_H:
                o_ref[case, 0, :, j * GRID_W:(j + 1) * GRID_W] = slab


def _ada_kernel(c_ref, w_ref, b_ref, rpb_ref, o_ref, bias_ref):
    j = pl.program_id(0)
    c = c_ref[...]
    a = (c * _sigmoid(c)).astype(BF16)
    o_ref[...] = _dot(a, w_ref[...].astype(BF16)) + b_ref[...]

    @pl.when(j < N_HEADS)
    def _():
        _write_bias(rpb_ref, j, bias_ref)


def _ada_and_bias(c_all, w_ada, b_ada, rpb):
    n = w_ada.shape[1]
    assert n // ADA_TN >= N_HEADS
    return pl.pallas_call(
        _ada_kernel,
        grid=(n // ADA_TN,),
        in_specs=[
            _const_spec((ADA_ROWS, D_MODEL)),
            pl.BlockSpec((D_MODEL, ADA_TN), lambda j: (0, j)),
            pl.BlockSpec((1, ADA_TN), lambda j: (0, j)),
            pl.BlockSpec(memory_space=pltpu.SMEM),
        ],
        out_specs=[
            pl.BlockSpec((ADA_ROWS, ADA_TN), lambda j: (0, j)),
            pl.BlockSpec((WIN_H, 1, GRID_W, WIN_KEYS), lambda j: (0, jnp.minimum(j, N_HEADS - 1), 0, 0)),
        ],
        out_shape=[
            jax.ShapeDtypeStruct((ADA_ROWS, n), F32),
            jax.ShapeDtypeStruct((WIN_H, N_HEADS, GRID_W, WIN_KEYS), F32),
        ],
        compiler_params=_params("arbitrary"),
        name="ada_bias",
    )(c_all, w_ada, b_ada, rpb)


def _ctx_kernel(ctx_ref, g_ref, modc_ref, wk_ref, wv_ref, kc_ref, vc_ref):
    y = _rmsnorm(ctx_ref[...], g_ref[...])
    h = (y * (1.0 + modc_ref[1:2, :]) + modc_ref[0:1, :]).astype(BF16)
    kc_ref[...] = _dot(h, wk_ref[...]).astype(BF16)
    vc_ref[...] = _dot(h, wv_ref[...]).astype(BF16)


def _ctx_kv(ctx2, g_pre, mod_c, w_qkv):
    m = ctx2.shape[0]
    return pl.pallas_call(
        _ctx_kernel,
        grid=(m // TM_CTX,),
        in_specs=[
            pl.BlockSpec((TM_CTX, D_MODEL), lambda i: (i, 0)),
            _const_spec((1, D_MODEL)),
            _const_spec((N_MOD, D_MODEL)),
            pl.BlockSpec((D_MODEL, ATT_W), lambda i: (0, K_OFF // ATT_W), pipeline_mode=pl.Buffered(1)),
            pl.BlockSpec((D_MODEL, ATT_W), lambda i: (0, V_OFF // ATT_W), pipeline_mode=pl.Buffered(1)),
        ],
        out_specs=[
            pl.BlockSpec((TM_CTX, ATT_W), lambda i: (i, 0)),
            pl.BlockSpec((TM_CTX, ATT_W), lambda i: (i, 0)),
        ],
        out_shape=[jax.ShapeDtypeStruct((m, ATT_W), BF16)] * 2,
        compiler_params=_params("arbitrary"),
        name="ctx_kv",
    )(ctx2, g_pre, mod_c, w_qkv, w_qkv)


def _in_kernel(x_ref, g_ref, mod_ref, w_ref, p_ref):
    y = _rmsnorm(x_ref[...], g_ref[...])
    h = (y * (1.0 + mod_ref[0, 1:2, :]) + mod_ref[0, 0:1, :]).astype(BF16)

    def proj(off):
        return _dot(h, w_ref[:, off:off + IN_CHUNK])

    def put(off, val):
        p_ref[:, off:off + IN_CHUNK] = val.astype(BF16)

    for n in range(0, ATT_W, IN_CHUNK):
        put(Q_OFF + n, proj(Q_OFF + n) * QK_SCALE)
        put(K_OFF + n, proj(K_OFF + n))
        put(V_OFF + n, proj(V_OFF + n))


def _in_proj(x2, g_pre, mod, w_qkv):
    m = x2.shape[0]
    tpb = SEQ // TM_IN
    return pl.pallas_call(
        _in_kernel,
        grid=(m // TM_IN,),
        in_specs=[
            pl.BlockSpec((TM_IN, D_MODEL), lambda i: (i, 0)),
            _const_spec((1, D_MODEL)),
            pl.BlockSpec((1, N_MOD, D_MODEL), lambda i: (i // tpb, 0, 0)),
            _const_spec((D_MODEL, P_W)),
        ],
        out_specs=pl.BlockSpec((TM_IN, P_W), lambda i: (i, 0)),
        out_shape=jax.ShapeDtypeStruct((m, P_W), BF16),
        compiler_params=_params("arbitrary"),
        name="in_proj",
    )(x2, g_pre, mod, w_qkv)


def _attn_kernel(q_ref, k_ref, v_ref, kc_ref, vc_ref, bias_ref, o_ref):
    lane = lax.broadcasted_iota(jnp.int32, (GRID_W, HEAD_PAIR_W), 1)
    first_head = lane < HEAD_DIM
    zero = jnp.zeros((GRID_W, HEAD_PAIR_W), BF16)
    ones = jnp.ones((ALL_KEYS, HEAD_PAIR_W), BF16)

    def offsets(r):
        rs = jnp.clip(r - WIN_H // 2, 0, ROWS - WIN_H)
        return r - rs, pl.multiple_of(r * GRID_W, GRID_W), pl.multiple_of(rs * GRID_W, GRID_W)

    def scores(r, pr):
        case, q0, k0 = offsets(r)
        cols = slice(pr * HEAD_PAIR_W, (pr + 1) * HEAD_PAIR_W)
        qp = q_ref[pl.ds(q0, GRID_W), cols]
        qm = jnp.concatenate([jnp.where(first_head, qp, zero), jnp.where(first_head, zero, qp)], axis=0)
        keys = jnp.concatenate([k_ref[pl.ds(k0, WIN_KEYS), cols], kc_ref[:, cols]], axis=0)
        s = _dot_nt(qm, keys)
        bias = jnp.concatenate([bias_ref[case, 2 * pr], bias_ref[case, 2 * pr + 1]], axis=0)
        return jnp.concatenate([s[:, :WIN_KEYS] + bias, s[:, WIN_KEYS:]], axis=1)

    def outputs(r, pr, s):
        _, q0, k0 = offsets(r)
        cols = slice(pr * HEAD_PAIR_W, (pr + 1) * HEAD_PAIR_W)
        e = jnp.exp2(s - jnp.max(s, axis=-1, keepdims=True)).astype(BF16)
        vals = jnp.concatenate([v_ref[pl.ds(k0, WIN_KEYS), cols], vc_ref[:, cols]], axis=0)
        o2 = _dot(e, jnp.concatenate([vals, ones], axis=1))
        o2 = o2[:, :HEAD_PAIR_W] / o2[:, HEAD_PAIR_W:]
        o = jnp.where(first_head, o2[:GRID_W], o2[GRID_W:])
        o_ref[pl.ds(q0, GRID_W), cols] = o.astype(BF16)

    def rows(i, carry):
        items = [(i * ATTN_ROWS_PER_STEP + dr, pr) for dr in range(ATTN_ROWS_PER_STEP) for pr in range(N_PAIRS)]
        pending = [scores(*it) for it in items[:ATTN_LOOKAHEAD]]
        for n, it in enumerate(items):
            if n + ATTN_LOOKAHEAD < len(items):
                pending.append(scores(*items[n + ATTN_LOOKAHEAD]))
            outputs(*it, pending.pop(0))
        return carry

    lax.fori_loop(0, ROWS // ATTN_ROWS_PER_STEP, rows, 0)


def _attention(p, kc, vc, bias):
    seq_spec = lambda col: pl.BlockSpec((SEQ, ATT_W), lambda b: (b, col))
    ctx_spec = pl.BlockSpec((CTX_LEN, ATT_W), lambda b: (b, 0))
    return pl.pallas_call(
        _attn_kernel,
        grid=(BATCH,),
        in_specs=[
            seq_spec(Q_OFF // ATT_W), seq_spec(K_OFF // ATT_W), seq_spec(V_OFF // ATT_W),
            ctx_spec, ctx_spec,
            _const_spec((WIN_H, N_HEADS, GRID_W, WIN_KEYS)),
        ],
        out_specs=pl.BlockSpec((SEQ, ATT_W), lambda b: (b, 0)),
        out_shape=jax.ShapeDtypeStruct((BATCH * SEQ, ATT_W), BF16),
        compiler_params=_params("arbitrary"),
        name="attention",
    )(p, p, p, kc, vc, bias)


def _merge_kernel(attn_ref, x_ref, xp_ref, xn_ref, mod_ref, gpre_ref, gpost_ref, wg_ref, wc_ref,
                  wba_ref, wbb_ref, wo_ref, x1_ref, hx_scr, m_scr):
    tm = TM_MERGE
    t = pl.program_id(0) % (SEQ // tm)
    g_in = gpre_ref[...] * (1.0 + mod_ref[0, 1:2, :])
    g_mix = gpost_ref[...] * mod_ref[0, 2:3, :]

    def normed(x):
        return (_rmsnorm(x, g_in) + mod_ref[0, 0:1, :]).astype(BF16)

    zeros = jnp.zeros((HALO, D_MODEL), BF16)
    hx_scr[0:HALO, :] = jnp.where(t > 0, normed(xp_ref[...]), zeros)
    hx_scr[HALO:HALO + tm, :] = normed(x_ref[...])
    hx_scr[HALO + tm:, :] = jnp.where(t < SEQ // tm - 1, normed(xn_ref[...]), zeros)
    m_scr[...] = (_dot(hx_scr[...], wg_ref[:, G_CG:G_CG + CONV_W])
                  * _dot(hx_scr[...], wg_ref[:, G_HV:G_HV + CONV_W]))

    def branches(r):
        rows = slice(r, r + SUB_ROWS)
        h = hx_scr[HALO + r:HALO + r + SUB_ROWS, :]
        conv = (wc_ref[0:1, :] * m_scr[HALO - 1 + r:HALO - 1 + r + SUB_ROWS, :]
                + wc_ref[1:2, :] * m_scr[HALO + r:HALO + r + SUB_ROWS, :]
                + wc_ref[2:3, :] * m_scr[HALO + 1 + r:HALO + 1 + r + SUB_ROWS, :])
        z = (_dot(h, wg_ref[:, G_BG:G_BG + CONV_W]) * conv).astype(BF16)
        gate_attn = _sigmoid(_dot(h, wg_ref[:, G_GA:G_GA + D_MODEL]))
        gate_conv = _sigmoid(_dot(h, wg_ref[:, G_GB:G_GB + D_MODEL]))
        y_attn = _dot(attn_ref[rows, :], wba_ref[...])
        y_conv = _dot(z, wbb_ref[...])
        return (gate_attn * y_attn + gate_conv * y_conv).astype(BF16)

    def project(r, mix):
        rows = slice(r, r + SUB_ROWS)
        x1_ref[rows, :] = x_ref[rows, :] + _rmsnorm(_dot(mix, wo_ref[...]), g_mix)

    starts = list(range(0, tm, SUB_ROWS))
    mix = branches(starts[0])
    for k, r in enumerate(starts):
        nxt = branches(starts[k + 1]) if k + 1 < len(starts) else None
        project(r, mix)
        mix = nxt


def _merge(attn, x2, mod, g_pre, g_post, w_gate, w_conv, w_ba, w_bb, w_o):
    m = x2.shape[0]
    tm = TM_MERGE
    tpb = SEQ // tm
    hb = tm // HALO
    n_halo = m // HALO
    return pl.pallas_call(
        _merge_kernel,
        grid=(m // tm,),
        in_specs=[
            pl.BlockSpec((tm, ATT_W), lambda i: (i, 0)),
            pl.BlockSpec((tm, D_MODEL), lambda i: (i, 0)),
            pl.BlockSpec((HALO, D_MODEL), lambda i: (jnp.maximum(i * hb - 1, 0), 0)),
            pl.BlockSpec((HALO, D_MODEL), lambda i: (jnp.minimum((i + 1) * hb, n_halo - 1), 0)),
            pl.BlockSpec((1, N_MOD, D_MODEL), lambda i: (i // tpb, 0, 0)),
            _const_spec((1, D_MODEL)), _const_spec((1, D_MODEL)),
            _const_spec((D_MODEL, GATE_W)),
            _const_spec((3, CONV_W)),
            _const_spec((ATT_W, D_MODEL)), _const_spec((CONV_W, D_MODEL)), _const_spec((D_MODEL, D_MODEL)),
        ],
        out_specs=pl.BlockSpec((tm, D_MODEL), lambda i: (i, 0)),
        out_shape=jax.ShapeDtypeStruct((m, D_MODEL), F32),
        scratch_shapes=[
            pltpu.VMEM((tm + 2 * HALO, D_MODEL), BF16),
            pltpu.VMEM((tm + 2 * HALO, CONV_W), F32),
        ],
        compiler_params=_params("arbitrary"),
        name="merge",
    )(attn, x2, x2, x2, mod, g_pre, g_post, w_gate, w_conv, w_ba, w_bb, w_o)


def _ffn_kernel(x1_ref, xp_ref, xn_ref, mod_ref, gpre_ref, gpost_ref, wup_ref, wc_ref, bc_ref, wdn_ref,
                o_ref, hx_scr, g_scr, act_scr):
    tm = TM_FFN
    t = pl.program_id(0) % (SEQ // tm)
    g_in = gpre_ref[...] * (1.0 + mod_ref[0, 4:5, :])

    def normed(x):
        return (_rmsnorm(x, g_in) + mod_ref[0, 3:4, :]).astype(BF16)

    zeros = jnp.zeros((HALO, D_MODEL), BF16)
    hx_scr[0:HALO, :] = jnp.where(t > 0, normed(xp_ref[...]), zeros)
    hx_scr[HALO:HALO + tm, :] = normed(x1_ref[...])
    hx_scr[HALO + tm:, :] = jnp.where(t < SEQ // tm - 1, normed(xn_ref[...]), zeros)
    for c, n in enumerate(range(0, D_FF, FF_CHUNK)):
        g = g_scr.at[c % 2]
        g[...] = _dot(hx_scr[...], wup_ref[:, D_FF + n:D_FF + n + FF_CHUNK])
        a = _dot(hx_scr[HALO:HALO + tm, :], wup_ref[:, n:n + FF_CHUNK])
        conv = (wc_ref[0:1, n:n + FF_CHUNK] * g[HALO - 1:HALO - 1 + tm, :]
                + wc_ref[1:2, n:n + FF_CHUNK] * g[HALO:HALO + tm, :]
                + wc_ref[2:3, n:n + FF_CHUNK] * g[HALO + 1:HALO + 1 + tm, :]
                + bc_ref[:, n:n + FF_CHUNK])
        act_scr[:, n:n + FF_CHUNK] = (_gelu_tanh(conv) * a).astype(BF16)
    g_out = gpost_ref[...] * mod_ref[0, 5:6, :]
    for r in range(0, tm, SUB_ROWS):
        rows = slice(r, r + SUB_ROWS)
        y = _dot(act_scr[rows, :], wdn_ref[...])
        o_ref[rows, :] = x1_ref[rows, :] + _rmsnorm(y, g_out)


def _ffn(x1, mod, g_pre, g_post, w_up, w_conv, b_conv, w_down):
    m = x1.shape[0]
    tm = TM_FFN
    tpb = SEQ // tm
    hb = tm // HALO
    n_halo = m // HALO
    return pl.pallas_call(
        _ffn_kernel,
        grid=(m // tm,),
        in_specs=[
            pl.BlockSpec((tm, D_MODEL), lambda i: (i, 0)),
            pl.BlockSpec((HALO, D_MODEL), lambda i: (jnp.maximum(i * hb - 1, 0), 0)),
            pl.BlockSpec((HALO, D_MODEL), lambda i: (jnp.minimum((i + 1) * hb, n_halo - 1), 0)),
            pl.BlockSpec((1, N_MOD, D_MODEL), lambda i: (i // tpb, 0, 0)),
            _const_spec((1, D_MODEL)),
            _const_spec((1, D_MODEL)),
            _const_spec((D_MODEL, 2 * D_FF)),
            _const_spec((3, D_FF)),
            _const_spec((1, D_FF)),
            _const_spec((D_FF, D_MODEL)),
        ],
        out_specs=pl.BlockSpec((tm, D_MODEL), lambda i: (i, 0)),
        out_shape=jax.ShapeDtypeStruct((m, D_MODEL), F32),
        scratch_shapes=[
            pltpu.VMEM((tm + 2 * HALO, D_MODEL), BF16),
            pltpu.VMEM((2, tm + 2 * HALO, FF_CHUNK), F32),
            pltpu.VMEM((tm, D_FF), BF16),
        ],
        compiler_params=_params("arbitrary"),
        name="conv_ffn",
    )(x1, x1, x1, mod, g_pre, g_post, w_up, w_conv, b_conv, w_down)


def kernel(x, c, ctx, c_ctx, w_ada, b_ada, g_pre_mix, g_post_mix, g_pre_ffn, g_post_ffn, w_in, rpb,
           w_short_conv, w_branch_attn, w_branch_conv, w_out, w_up, w_ffn_conv, b_ffn_conv, w_down):
    depth = w_in.shape[0]
    assert depth == 1, "single-layer configuration"
    b, s, d = x.shape
    x2 = x.reshape(b * s, d)
    ctx2 = ctx.reshape(b * CTX_LEN, d)

    c_all = jnp.concatenate([c, c_ctx[None, :], jnp.zeros((ADA_ROWS - b - 1, d), F32)], axis=0)
    mod_all, bias = _ada_and_bias(c_all, w_ada[0], b_ada[0][None, :], rpb[0])
    mod = mod_all[:b].reshape(b, N_MOD, d)
    mod_c = mod_all[b].reshape(N_MOD, d)

    w_qkv = w_in[0][:, :BG_OFF].astype(BF16)
    w_gate = w_in[0][:, BG_OFF:].astype(BF16)
    kc, vc = _ctx_kv(ctx2, g_pre_mix[0][None, :], mod_c, w_qkv)
    p = _in_proj(x2, g_pre_mix[0][None, :], mod, w_qkv)
    attn = _attention(p, kc, vc, bias)
    x1 = _merge(attn, x2, mod, g_pre_mix[0][None, :], g_post_mix[0][None, :], w_gate, w_short_conv[0],
                w_branch_attn[0].astype(BF16), w_branch_conv[0].astype(BF16), w_out[0].astype(BF16))
    out = _ffn(x1, mod, g_pre_ffn[0][None, :], g_post_ffn[0][None, :], w_up[0].astype(BF16), w_ffn_conv[0],
               b_ffn_conv[0][None, :], w_down[0].astype(BF16))
    return out.reshape(b, s, d)
```

```python
import jax
import jax.numpy as jnp
from jax import lax
from jax.experimental import pallas as pl
from jax.experimental.pallas import tpu as pltpu

F32 = jnp.float32
BF16 = jnp.bfloat16

D_MODEL = 1024
BATCH = 16
SEQ = 2048
CTX_LEN = 256
GRID_W = 64
ROWS = SEQ // GRID_W
N_HEADS = 8
HEAD_DIM = 64
ATT_W = N_HEADS * HEAD_DIM
CONV_W = 512
WIN_H = 8
WIN_W = 16
D_FF = 2816
EPS = 1e-6
NEG_INF = -1e30

Q_OFF = 0
K_OFF = ATT_W
V_OFF = 2 * ATT_W
BG_OFF = 3 * ATT_W
CG_OFF = BG_OFF + CONV_W
HV_OFF = CG_OFF + CONV_W
GA_OFF = HV_OFF + CONV_W
GB_OFF = GA_OFF + D_MODEL
IN_W = GB_OFF + D_MODEL

P_W = BG_OFF
GATE_W = IN_W - BG_OFF
G_BG = 0
G_CG = CG_OFF - BG_OFF
G_HV = HV_OFF - BG_OFF
G_GA = GA_OFF - BG_OFF
G_GB = GB_OFF - BG_OFF

N_MOD = 6
ADA_ROWS = 24
HALO = 16
HEAD_PAIR_W = 2 * HEAD_DIM
N_PAIRS = N_HEADS // 2
WIN_KEYS = WIN_H * GRID_W
ALL_KEYS = WIN_KEYS + CTX_LEN
LOG2_E = 1.4426950408889634
QK_SCALE = HEAD_DIM ** -0.5 * LOG2_E

VMEM_LIMIT_BYTES = 56 * 1024 * 1024

ADA_TN = 512
TM_IN = 2048
TM_CTX = 512
TM_MERGE = 1024
TM_FFN = 1024
IN_CHUNK = 512
FF_CHUNK = 256
SUB_ROWS = 256
ATTN_ROWS_PER_STEP = 8
ATTN_LOOKAHEAD = 3


def _params(*sem):
    return pltpu.CompilerParams(dimension_semantics=sem, vmem_limit_bytes=VMEM_LIMIT_BYTES)


def _const_spec(shape):
    nd = len(shape)
    return pl.BlockSpec(shape, lambda *_: (0,) * nd, pipeline_mode=pl.Buffered(1))


def _dot(a, b):
    return jnp.dot(a, b, preferred_element_type=F32)


def _dot_nt(a, b):
    return lax.dot_general(a, b, (((1,), (1,)), ((), ())), preferred_element_type=F32)


def _sigmoid(x):
    return 1.0 / (1.0 + jnp.exp(-x))


def _rmsnorm(x, g):
    ms = jnp.mean(x * x, axis=-1, keepdims=True)
    return x * lax.rsqrt(ms + EPS) * g


def _gelu_tanh(x):
    c = 0.7978845608028654
    return x * (0.5 * (1.0 + jnp.tanh(c * (x + 0.044715 * (x * x * x)))))


def _write_bias(rpb_ref, h, o_ref):
    qcol = lax.broadcasted_iota(jnp.int32, (GRID_W, GRID_W), 0)
    kcol = lax.broadcasted_iota(jnp.int32, (GRID_W, GRID_W), 1)
    cs = jnp.clip(qcol - WIN_W // 2, 0, GRID_W - WIN_W)
    valid = (kcol >= cs) & (kcol < cs + WIN_W)
    dc = jnp.clip(kcol - qcol + WIN_W - 1, 0, 2 * WIN_W - 2)
    for dr in range(2 * WIN_H - 1):
        slab = jnp.zeros((GRID_W, GRID_W), F32)
        for k in range(2 * WIN_W - 1):
            slab = jnp.where(dc == k, rpb_ref[h, dr, k], slab)
        slab = jnp.where(valid, slab * LOG2_E, NEG_INF)
        for case in range(WIN_H):
            j = dr - (WIN_H - 1) + case
            if 0 <= j < WIN_H:
                o_ref[case, 0, :, j * GRID_W:(j + 1) * GRID_W] = slab


def _ada_kernel(c_ref, w_ref, b_ref, rpb_ref, o_ref, bias_ref):
    j = pl.program_id(0)
    c = c_ref[...]
    a = (c * _sigmoid(c)).astype(BF16)
    o_ref[...] = _dot(a, w_ref[...].astype(BF16)) + b_ref[...]

    @pl.when(j < N_HEADS)
    def _():
        _write_bias(rpb_ref, j, bias_ref)


def _ada_and_bias(c_all, w_ada, b_ada, rpb):
    n = w_ada.shape[1]
    assert n // ADA_TN >= N_HEADS
    return pl.pallas_call(
        _ada_kernel,
        grid=(n // ADA_TN,),
        in_specs=[
            _const_spec((ADA_ROWS, D_MODEL)),
            pl.BlockSpec((D_MODEL, ADA_TN), lambda j: (0, j)),
            pl.BlockSpec((1, ADA_TN), lambda j: (0, j)),
            pl.BlockSpec(memory_space=pltpu.SMEM),
        ],
        out_specs=[
            pl.BlockSpec((ADA_ROWS, ADA_TN), lambda j: (0, j)),
            pl.BlockSpec((WIN_H, 1, GRID_W, WIN_KEYS), lambda j: (0, jnp.minimum(j, N_HEADS - 1), 0, 0)),
        ],
        out_shape=[
            jax.ShapeDtypeStruct((ADA_ROWS, n), F32),
            jax.ShapeDtypeStruct((WIN_H, N_HEADS, GRID_W, WIN_KEYS), F32),
        ],
        compiler_params=_params("arbitrary"),
        name="ada_bias",
    )(c_all, w_ada, b_ada, rpb)


def _ctx_kernel(ctx_ref, g_ref, modc_ref, wk_ref, wv_ref, kc_ref, vc_ref):
    y = _rmsnorm(ctx_ref[...], g_ref[...])
    h = (y * (1.0 + modc_ref[1:2, :]) + modc_ref[0:1, :]).astype(BF16)
    kc_ref[...] = _dot(h, wk_ref[...]).astype(BF16)
    vc_ref[...] = _dot(h, wv_ref[...]).astype(BF16)


def _ctx_kv(ctx2, g_pre, mod_c, w_qkv):
    m = ctx2.shape[0]
    return pl.pallas_call(
        _ctx_kernel,
        grid=(m // TM_CTX,),
        in_specs=[
            pl.BlockSpec((TM_CTX, D_MODEL), lambda i: (i, 0)),
            _const_spec((1, D_MODEL)),
            _const_spec((N_MOD, D_MODEL)),
            pl.BlockSpec((D_MODEL, ATT_W), lambda i: (0, K_OFF // ATT_W), pipeline_mode=pl.Buffered(1)),
            pl.BlockSpec((D_MODEL, ATT_W), lambda i: (0, V_OFF // ATT_W), pipeline_mode=pl.Buffered(1)),
        ],
        out_specs=[
            pl.BlockSpec((TM_CTX, ATT_W), lambda i: (i, 0)),
            pl.BlockSpec((TM_CTX, ATT_W), lambda i: (i, 0)),
        ],
        out_shape=[jax.ShapeDtypeStruct((m, ATT_W), BF16)] * 2,
        compiler_params=_params("arbitrary"),
        name="ctx_kv",
    )(ctx2, g_pre, mod_c, w_qkv, w_qkv)


def _in_kernel(x_ref, g_ref, mod_ref, w_ref, p_ref):
    y = _rmsnorm(x_ref[...], g_ref[...])
    h = (y * (1.0 + mod_ref[0, 1:2, :]) + mod_ref[0, 0:1, :]).astype(BF16)

    def proj(off):
        return _dot(h, w_ref[:, off:off + IN_CHUNK])

    def put(off, val):
        p_ref[:, off:off + IN_CHUNK] = val.astype(BF16)

    for n in range(0, ATT_W, IN_CHUNK):
        put(Q_OFF + n, proj(Q_OFF + n) * QK_SCALE)
        put(K_OFF + n, proj(K_OFF + n))
        put(V_OFF + n, proj(V_OFF + n))


def _in_proj(x2, g_pre, mod, w_qkv):
    m = x2.shape[0]
    tpb = SEQ // TM_IN
    return pl.pallas_call(
        _in_kernel,
        grid=(m // TM_IN,),
        in_specs=[
            pl.BlockSpec((TM_IN, D_MODEL), lambda i: (i, 0)),
            _const_spec((1, D_MODEL)),
            pl.BlockSpec((1, N_MOD, D_MODEL), lambda i: (i // tpb, 0, 0)),
            _const_spec((D_MODEL, P_W)),
        ],
        out_specs=pl.BlockSpec((TM_IN, P_W), lambda i: (i, 0)),
        out_shape=jax.ShapeDtypeStruct((m, P_W), BF16),
        compiler_params=_params("arbitrary"),
        name="in_proj",
    )(x2, g_pre, mod, w_qkv)


def _attn_kernel(q_ref, k_ref, v_ref, kc_ref, vc_ref, bias_ref, o_ref):
    lane = lax.broadcasted_iota(jnp.int32, (GRID_W, HEAD_PAIR_W), 1)
    first_head = lane < HEAD_DIM
    zero = jnp.zeros((GRID_W, HEAD_PAIR_W), BF16)
    ones = jnp.ones((ALL_KEYS, HEAD_PAIR_W), BF16)

    def offsets(r):
        rs = jnp.clip(r - WIN_H // 2, 0, ROWS - WIN_H)
        return r - rs, pl.multiple_of(r * GRID_W, GRID_W), pl.multiple_of(rs * GRID_W, GRID_W)

    def scores(r, pr):
        case, q0, k0 = offsets(r)
        cols = slice(pr * HEAD_PAIR_W, (pr + 1) * HEAD_PAIR_W)
        qp = q_ref[pl.ds(q0, GRID_W), cols]
        qm = jnp.concatenate([jnp.where(first_head, qp, zero), jnp.where(first_head, zero, qp)], axis=0)
        keys = jnp.concatenate([k_ref[pl.ds(k0, WIN_KEYS), cols], kc_ref[:, cols]], axis=0)
        s = _dot_nt(qm, keys)
        bias = jnp.concatenate([bias_ref[case, 2 * pr], bias_ref[case, 2 * pr + 1]], axis=0)
        return jnp.concatenate([s[:, :WIN_KEYS] + bias, s[:, WIN_KEYS:]], axis=1)

    def outputs(r, pr, s):
        _, q0, k0 = offsets(r)
        cols = slice(pr * HEAD_PAIR_W, (pr + 1) * HEAD_PAIR_W)
        e = jnp.exp2(s - jnp.max(s, axis=-1, keepdims=True)).astype(BF16)
        vals = jnp.concatenate([v_ref[pl.ds(k0, WIN_KEYS), cols], vc_ref[:, cols]], axis=0)
        o2 = _dot(e, jnp.concatenate([vals, ones], axis=1))
        o2 = o2[:, :HEAD_PAIR_W] / o2[:, HEAD_PAIR_W:]
        o = jnp.where(first_head, o2[:GRID_W], o2[GRID_W:])
        o_ref[pl.ds(q0, GRID_W), cols] = o.astype(BF16)

    def rows(i, carry):
        items = [(i * ATTN_ROWS_PER_STEP + dr, pr) for dr in range(ATTN_ROWS_PER_STEP) for pr in range(N_PAIRS)]
        pending = [scores(*it) for it in items[:ATTN_LOOKAHEAD]]
        for n, it in enumerate(items):
            if n + ATTN_LOOKAHEAD < len(items):
                pending.append(scores(*items[n + ATTN_LOOKAHEAD]))
            outputs(*it, pending.pop(0))
        return carry

    lax.fori_loop(0, ROWS // ATTN_ROWS_PER_STEP, rows, 0)


def _attention(p, kc, vc, bias):
    seq_spec = lambda col: pl.BlockSpec((SEQ, ATT_W), lambda b: (b, col))
    ctx_spec = pl.BlockSpec((CTX_LEN, ATT_W), lambda b: (b, 0))
    return pl.pallas_call(
        _attn_kernel,
        grid=(BATCH,),
        in_specs=[
            seq_spec(Q_OFF // ATT_W), seq_spec(K_OFF // ATT_W), seq_spec(V_OFF // ATT_W),
            ctx_spec, ctx_spec,
            _const_spec((WIN_H, N_HEADS, GRID_W, WIN_KEYS)),
        ],
        out_specs=pl.BlockSpec((SEQ, ATT_W), lambda b: (b, 0)),
        out_shape=jax.ShapeDtypeStruct((BATCH * SEQ, ATT_W), BF16),
        compiler_params=_params("arbitrary"),
        name="attention",
    )(p, p, p, kc, vc, bias)


def _merge_kernel(attn_ref, x_ref, xp_ref, xn_ref, mod_ref, gpre_ref, gpost_ref, wg_ref, wc_ref,
                  wba_ref, wbb_ref, wo_ref, x1_ref, hx_scr, m_scr):
    tm = TM_MERGE
    t = pl.program_id(0) % (SEQ // tm)
    g_in = gpre_ref[...] * (1.0 + mod_ref[0, 1:2, :])
    g_mix = gpost_ref[...] * mod_ref[0, 2:3, :]

    def normed(x):
        return (_rmsnorm(x, g_in) + mod_ref[0, 0:1, :]).astype(BF16)

    zeros = jnp.zeros((HALO, D_MODEL), BF16)
    hx_scr[0:HALO, :] = jnp.where(t > 0, normed(xp_ref[...]), zeros)
    hx_scr[HALO:HALO + tm, :] = normed(x_ref[...])
    hx_scr[HALO + tm:, :] = jnp.where(t < SEQ // tm - 1, normed(xn_ref[...]), zeros)
    m_scr[...] = (_dot(hx_scr[...], wg_ref[:, G_CG:G_CG + CONV_W])
                  * _dot(hx_scr[...], wg_ref[:, G_HV:G_HV + CONV_W]))

    def branches(r):
        rows = slice(r, r + SUB_ROWS)
        h = hx_scr[HALO + r:HALO + r + SUB_ROWS, :]
        conv = (wc_ref[0:1, :] * m_scr[HALO - 1 + r:HALO - 1 + r + SUB_ROWS, :]
                + wc_ref[1:2, :] * m_scr[HALO + r:HALO + r + SUB_ROWS, :]
                + wc_ref[2:3, :] * m_scr[HALO + 1 + r:HALO + 1 + r + SUB_ROWS, :])
        gate_attn = _sigmoid(_dot(h, wg_ref[:, G_GA:G_GA + D_MODEL]))
        gate_conv = _sigmoid(_dot(h, wg_ref[:, G_GB:G_GB + D_MODEL]))
        z = (_dot(h, wg_ref[:, G_BG:G_BG + CONV_W]) * conv).astype(BF16)
        y_attn = _dot(attn_ref[rows, :], wba_ref[...])
        y_conv = _dot(z, wbb_ref[...])
        return (gate_attn * y_attn + gate_conv * y_conv).astype(BF16)

    def project(r, mix):
        rows = slice(r, r + SUB_ROWS)
        x1_ref[rows, :] = x_ref[rows, :] + _rmsnorm(_dot(mix, wo_ref[...]), g_mix)

    starts = list(range(0, tm, SUB_ROWS))
    mix = branches(starts[0])
    for k, r in enumerate(starts):
        nxt = branches(starts[k + 1]) if k + 1 < len(starts) else None
        project(r, mix)
        mix = nxt


def _merge(attn, x2, mod, g_pre, g_post, w_gate, w_conv, w_ba, w_bb, w_o):
    m = x2.shape[0]
    tm = TM_MERGE
    tpb = SEQ // tm
    hb = tm // HALO
    n_halo = m // HALO
    return pl.pallas_call(
        _merge_kernel,
        grid=(m // tm,),
        in_specs=[
            pl.BlockSpec((tm, ATT_W), lambda i: (i, 0)),
            pl.BlockSpec((tm, D_MODEL), lambda i: (i, 0)),
            pl.BlockSpec((HALO, D_MODEL), lambda i: (jnp.maximum(i * hb - 1, 0), 0)),
            pl.BlockSpec((HALO, D_MODEL), lambda i: (jnp.minimum((i + 1) * hb, n_halo - 1), 0)),
            pl.BlockSpec((1, N_MOD, D_MODEL), lambda i: (i // tpb, 0, 0)),
            _const_spec((1, D_MODEL)), _const_spec((1, D_MODEL)),
            _const_spec((D_MODEL, GATE_W)),
            _const_spec((3, CONV_W)),
            _const_spec((ATT_W, D_MODEL)), _const_spec((CONV_W, D_MODEL)), _const_spec((D_MODEL, D_MODEL)),
        ],
        out_specs=pl.BlockSpec((tm, D_MODEL), lambda i: (i, 0)),
        out_shape=jax.ShapeDtypeStruct((m, D_MODEL), F32),
        scratch_shapes=[
            pltpu.VMEM((tm + 2 * HALO, D_MODEL), BF16),
            pltpu.VMEM((tm + 2 * HALO, CONV_W), F32),
        ],
        compiler_params=_params("arbitrary"),
        name="merge",
    )(attn, x2, x2, x2, mod, g_pre, g_post, w_gate, w_conv, w_ba, w_bb, w_o)


def _ffn_kernel(x1_ref, xp_ref, xn_ref, mod_ref, gpre_ref, gpost_ref, wup_ref, wc_ref, bc_ref, wdn_ref,
                o_ref, hx_scr, g_scr, act_scr):
    tm = TM_FFN
    t = pl.program_id(0) % (SEQ // tm)
    g_in = gpre_ref[...] * (1.0 + mod_ref[0, 4:5, :])

    def normed(x):
        return (_rmsnorm(x, g_in) + mod_ref[0, 3:4, :]).astype(BF16)

    zeros = jnp.zeros((HALO, D_MODEL), BF16)
    hx_scr[0:HALO, :] = jnp.where(t > 0, normed(xp_ref[...]), zeros)
    hx_scr[HALO:HALO + tm, :] = normed(x1_ref[...])
    hx_scr[HALO + tm:, :] = jnp.where(t < SEQ // tm - 1, normed(xn_ref[...]), zeros)
    for c, n in enumerate(range(0, D_FF, FF_CHUNK)):
        g = g_scr.at[c % 2]
        g[...] = _dot(hx_scr[...], wup_ref[:, D_FF + n:D_FF + n + FF_CHUNK])
        a = _dot(hx_scr[HALO:HALO + tm, :], wup_ref[:, n:n + FF_CHUNK])
        conv = (wc_ref[0:1, n:n + FF_CHUNK] * g[HALO - 1:HALO - 1 + tm, :]
                + wc_ref[1:2, n:n + FF_CHUNK] * g[HALO:HALO + tm, :]
                + wc_ref[2:3, n:n + FF_CHUNK] * g[HALO + 1:HALO + 1 + tm, :]
                + bc_ref[:, n:n + FF_CHUNK])
        act_scr[:, n:n + FF_CHUNK] = (_gelu_tanh(conv) * a).astype(BF16)
    g_out = gpost_ref[...] * mod_ref[0, 5:6, :]
    for r in range(0, tm, SUB_ROWS):
        rows = slice(r, r + SUB_ROWS)
        y = _dot(act_scr[rows, :], wdn_ref[...])
        o_ref[rows, :] = x1_ref[rows, :] + _rmsnorm(y, g_out)


def _ffn(x1, mod, g_pre, g_post, w_up, w_conv, b_conv, w_down):
    m = x1.shape[0]
    tm = TM_FFN
    tpb = SEQ // tm
    hb = tm // HALO
    n_halo = m // HALO
    return pl.pallas_call(
        _ffn_kernel,
        grid=(m // tm,),
        in_specs=[
            pl.BlockSpec((tm, D_MODEL), lambda i: (i, 0)),
            pl.BlockSpec((HALO, D_MODEL), lambda i: (jnp.maximum(i * hb - 1, 0), 0)),
            pl.BlockSpec((HALO, D_MODEL), lambda i: (jnp.minimum((i + 1) * hb, n_halo - 1), 0)),
            pl.BlockSpec((1, N_MOD, D_MODEL), lambda i: (i // tpb, 0, 0)),
            _const_spec((1, D_MODEL)),
            _const_spec((1, D_MODEL)),
            _const_spec((D_MODEL, 2 * D_FF)),
            _const_spec((3, D_FF)),
            _const_spec((1, D_FF)),
            _const_spec((D_FF, D_MODEL)),
        ],
        out_specs=pl.BlockSpec((tm, D_MODEL), lambda i: (i, 0)),
        out_shape=jax.ShapeDtypeStruct((m, D_MODEL), F32),
        scratch_shapes=[
            pltpu.VMEM((tm + 2 * HALO, D_MODEL), BF16),
            pltpu.VMEM((2, tm + 2 * HALO, FF_CHUNK), F32),
            pltpu.VMEM((tm, D_FF), BF16),
        ],
        compiler_params=_params("arbitrary"),
        name="conv_ffn",
    )(x1, x1, x1, mod, g_pre, g_post, w_up, w_conv, b_conv, w_down)


def kernel(x, c, ctx, c_ctx, w_ada, b_ada, g_pre_mix, g_post_mix, g_pre_ffn, g_post_ffn, w_in, rpb,
           w_short_conv, w_branch_attn, w_branch_conv, w_out, w_up, w_ffn_conv, b_ffn_conv, w_down):
    depth = w_in.shape[0]
    assert depth == 1, "single-layer configuration"
    b, s, d = x.shape
    x2 = x.reshape(b * s, d)
    ctx2 = ctx.reshape(b * CTX_LEN, d)

    c_all = jnp.concatenate([c, c_ctx[None, :], jnp.zeros((ADA_ROWS - b - 1, d), F32)], axis=0)
    mod_all, bias = _ada_and_bias(c_all, w_ada[0], b_ada[0][None, :], rpb[0])
    mod = mod_all[:b].reshape(b, N_MOD, d)
    mod_c = mod_all[b].reshape(N_MOD, d)

    w_qkv = w_in[0][:, :BG_OFF].astype(BF16)
    w_gate = w_in[0][:, BG_OFF:].astype(BF16)
    kc, vc = _ctx_kv(ctx2, g_pre_mix[0][None, :], mod_c, w_qkv)
    p = _in_proj(x2, g_pre_mix[0][None, :], mod, w_qkv)
    attn = _attention(p, kc, vc, bias)
    x1 = _merge(attn, x2, mod, g_pre_mix[0][None, :], g_post_mix[0][None, :], w_gate, w_short_conv[0],
                w_branch_attn[0].astype(BF16), w_branch_conv[0].astype(BF16), w_out[0].astype(BF16))
    out = _ffn(x1, mod, g_pre_ffn[0][None, :], g_post_ffn[0][None, :], w_up[0].astype(BF16), w_ffn_conv[0],
               b_ffn_conv[0][None, :], w_down[0].astype(BF16))
    return out.reshape(b, s, d)
```
